```python
import jax, jax.numpy as jnp
from jax import lax
import numpy as np


D_MODEL = 1024
BATCH = 8
SEQ = 4096
DEPTH = 1
DEC_BATCH = 32
DEC_SEQ = 8
PAST_LEN = 16384
PAGE_SIZE = 128

FOX_HEADS = 8
FOX_HEAD_DIM = 64
FOX_WIDTH = FOX_HEADS * FOX_HEAD_DIM
MLSTM_HEADS = 4
MLSTM_HEAD_DIM = 128
MLSTM_WIDTH = MLSTM_HEADS * MLSTM_HEAD_DIM
MIX_WIDTH = FOX_WIDTH + MLSTM_WIDTH
CONV_WIDTH = 4
D_FF = 4 * D_MODEL
Q_BLOCK = 128
MLSTM_CHUNK = 64
LN_EPS = 1e-5
DN_ALPHA = (2 * DEPTH) ** 0.25
DN_BETA = (8 * DEPTH) ** -0.25
FOX_FORGET_BIAS = 2.0

OFF_FOX_Q = 0
OFF_FOX_K = FOX_WIDTH
OFF_FOX_V = 2 * FOX_WIDTH
OFF_FOX_F = 3 * FOX_WIDTH
OFF_M_QK = OFF_FOX_F + FOX_HEADS
OFF_M_V = OFF_M_QK + 2 * MLSTM_WIDTH
OFF_M_I = OFF_M_V + MLSTM_WIDTH
OFF_M_F = OFF_M_I + MLSTM_HEADS
OFF_M_O = OFF_M_F + MLSTM_HEADS
IN_COLS = OFF_M_O + MLSTM_WIDTH

kernel_name = 'fox_mlstm_parallel_heads_deepnorm_step'


def _layer_norm(x, g, b):
    xf = x.astype(jnp.float32)
    mu = jnp.mean(xf, -1, keepdims=True)
    var = jnp.mean(jnp.square(xf - mu), -1, keepdims=True)
    return ((xf - mu) * lax.rsqrt(var + LN_EPS) * g + b).astype(x.dtype)


def _mixer_inputs(x, conv_prev, w_in, b_fox_f, b_ig, b_fg, b_og, conv_w, conv_b):
    B, S, _ = x.shape
    z = jnp.einsum('bsd,dc->bsc', x, w_in)
    fq = z[..., OFF_FOX_Q:OFF_FOX_K].reshape(B, S, FOX_HEADS, FOX_HEAD_DIM)
    fk = z[..., OFF_FOX_K:OFF_FOX_V].reshape(B, S, FOX_HEADS, FOX_HEAD_DIM)
    fv = z[..., OFF_FOX_V:OFF_FOX_F].reshape(B, S, FOX_HEADS, FOX_HEAD_DIM)
    flogf = jax.nn.log_sigmoid((z[..., OFF_FOX_F:OFF_M_QK] + b_fox_f).astype(jnp.float32))
    qk_in = z[..., OFF_M_QK:OFF_M_V]
    xpad = jnp.concatenate([conv_prev.astype(qk_in.dtype), qk_in], axis=1)
    conv = conv_b + xpad[:, 0:S] * conv_w[0]
    for j in range(1, CONV_WIDTH):
        conv = conv + xpad[:, j:j + S] * conv_w[j]
    qk = jax.nn.silu(conv)
    mq = qk[..., :MLSTM_WIDTH].reshape(B, S, MLSTM_HEADS, MLSTM_HEAD_DIM)
    mk = qk[..., MLSTM_WIDTH:].reshape(B, S, MLSTM_HEADS, MLSTM_HEAD_DIM) * (MLSTM_HEAD_DIM ** -0.5)
    mv = z[..., OFF_M_V:OFF_M_I].reshape(B, S, MLSTM_HEADS, MLSTM_HEAD_DIM)
    ig = (z[..., OFF_M_I:OFF_M_F] + b_ig).astype(jnp.float32)
    lf = jax.nn.log_sigmoid((z[..., OFF_M_F:OFF_M_O] + b_fg).astype(jnp.float32))
    og = jax.nn.sigmoid(z[..., OFF_M_O:] + b_og)
    new_conv = xpad[:, S:]
    return fq, fk, fv, flogf, mq, mk, mv, ig, lf, og, new_conv


def _fox_prompt(q, k, v, logf):
    B, S, H, Dh = q.shape
    nb = S // Q_BLOCK
    L = jnp.cumsum(logf, axis=1).transpose(0, 2, 1)
    qb = q.reshape(B, nb, Q_BLOCK, H, Dh).swapaxes(0, 1)
    Lb = L.reshape(B, H, nb, Q_BLOCK).transpose(2, 0, 1, 3)
    k_pos = jnp.arange(S)
    scale = Dh ** -0.5

    def block(args):
        q_i, L_i, i = args
        s = jnp.einsum('bqhd,bkhd->bhqk', q_i, k).astype(jnp.float32) * scale
        s = s + L_i[..., :, None] - L[:, :, None, :]
        q_pos = i * Q_BLOCK + jnp.arange(Q_BLOCK)
        s = jnp.where(k_pos[None, :] <= q_pos[:, None], s, -jnp.inf)
        p = jax.nn.softmax(s, axis=-1)
        return jnp.einsum('bhqk,bkhd->bqhd', p.astype(v.dtype), v)

    o = lax.map(block, (qb, Lb, jnp.arange(nb)))
    return o.swapaxes(0, 1).reshape(B, S, H * Dh)


def _fox_sample(q, k_new, v_new, logf_new, k_past, v_past, logf_past):
    B, T, H, Dh = q.shape
    P = k_past.shape[1]
    scale = Dh ** -0.5
    Pc = jnp.cumsum(logf_past.astype(jnp.float32), axis=1)
    R = (Pc[:, -1:, :] - Pc).transpose(0, 2, 1)
    Ln = jnp.cumsum(logf_new, axis=1).transpose(0, 2, 1)
    s_past = jnp.einsum('bqhd,bkhd->bhqk', q, k_past).astype(jnp.float32) * scale
    s_past = s_past + Ln[..., :, None] + R[:, :, None, :]
    s_new = jnp.einsum('bqhd,bkhd->bhqk', q, k_new).astype(jnp.float32) * scale
    s_new = s_new + Ln[..., :, None] - Ln[..., None, :]
    causal = jnp.tril(jnp.ones((T, T), dtype=bool))
    s_new = jnp.where(causal, s_new, -jnp.inf)
    p = jax.nn.softmax(jnp.concatenate([s_past, s_new], axis=-1), axis=-1).astype(v_new.dtype)
    o = jnp.einsum('bhqk,bkhd->bqhd', p[..., :P], v_past) + jnp.einsum('bhqk,bkhd->bqhd', p[..., P:], v_new)
    return o.reshape(B, T, H * Dh)


def _mlstm_chunk(carry, inp):
    C, n, m = carry
    q, k, v, ig, lf = inp
    q = q.astype(jnp.float32)
    k = k.astype(jnp.float32)
    v = v.astype(jnp.float32)
    L = q.shape[1]
    b = jnp.cumsum(lf, axis=1).transpose(0, 2, 1)
    i_g = ig.transpose(0, 2, 1)
    causal = jnp.tril(jnp.ones((L, L), dtype=bool))
    Dm = jnp.where(causal, b[..., :, None] - b[..., None, :] + i_g[..., None, :], -jnp.inf)
    inter = m[..., None] + b
    m_t = jnp.maximum(inter, jnp.max(Dm, axis=-1))
    W = jnp.exp(Dm - m_t[..., None])
    a = jnp.exp(inter - m_t)
    Wqk = W * jnp.einsum('blhd,bshd->bhls', q, k)
    num = jnp.einsum('bhls,bshe->blhe', Wqk, v) + jnp.einsum('blhd,bhde->blhe', q, C) * a.transpose(0, 2, 1)[..., None]
    nq = jnp.sum(Wqk, axis=-1) + jnp.einsum('blhd,bhd->bhl', q, n) * a
    den = jnp.maximum(jnp.abs(nq), jnp.exp(-m_t)).transpose(0, 2, 1)[..., None]
    h = num / den
    m_end = m_t[..., -1]
    g_state = jnp.exp(inter[..., -1] - m_end)
    w_key = jnp.exp(b[..., -1:] - b + i_g - m_end[..., None])
    C_new = g_state[..., None, None] * C + jnp.einsum('bhs,bshd,bshe->bhde', w_key, k, v)
    n_new = g_state[..., None] * n + jnp.einsum('bhs,bshd->bhd', w_key, k)
    return (C_new, n_new, m_end), h


def _mlstm_prompt(q, k, v, ig, lf):
    B, S = q.shape[0], q.shape[1]
    nc = S // MLSTM_CHUNK

    def to_chunks(a):
        return a.reshape((B, nc, MLSTM_CHUNK) + a.shape[2:]).swapaxes(0, 1)

    init = (jnp.zeros((B, MLSTM_HEADS, MLSTM_HEAD_DIM, MLSTM_HEAD_DIM), jnp.float32),
            jnp.zeros((B, MLSTM_HEADS, MLSTM_HEAD_DIM), jnp.float32),
            jnp.zeros((B, MLSTM_HEADS), jnp.float32))
    (C, n, m), h = lax.scan(_mlstm_chunk, init, (to_chunks(q), to_chunks(k), to_chunks(v), to_chunks(ig), to_chunks(lf)))
    h = h.swapaxes(0, 1).reshape(B, S, MLSTM_HEADS, MLSTM_HEAD_DIM)
    return (C, n, m), h


def _block_out(x, fox_h, m_h, og, mlstm_norm_w, w_o, ln1_g, ln1_b, w1, w2, ln2_g, ln2_b):
    B, S, _ = x.shape
    mu = jnp.mean(m_h, -1, keepdims=True)
    var = jnp.mean(jnp.square(m_h - mu), -1, keepdims=True)
    mn = ((m_h - mu) * lax.rsqrt(var + LN_EPS)).reshape(B, S, MLSTM_WIDTH) * mlstm_norm_w * og
    mix = jnp.concatenate([fox_h, mn.astype(x.dtype)], axis=-1) @ w_o
    x1 = _layer_norm(DN_ALPHA * x + mix, ln1_g, ln1_b)
    hid = jnp.square(jax.nn.relu(x1 @ w1))
    return _layer_norm(DN_ALPHA * x1 + hid @ w2, ln2_g, ln2_b)


def setup_inputs(seed: int = 0) -> dict:
    key = jax.random.key(seed)
    ks = jax.random.split(key, 32)
    nrm = jax.random.normal
    n_pages = PAST_LEN // PAGE_SIZE
    n_used = DEC_BATCH * n_pages
    n_pool = n_used + n_used // 4
    x_prompt = nrm(ks[0], (BATCH, SEQ, D_MODEL), jnp.float32)
    x_sample = nrm(ks[1], (DEC_BATCH, DEC_SEQ, D_MODEL), jnp.float32)
    cache_k = nrm(ks[2], (DEPTH, n_pool, PAGE_SIZE, FOX_HEADS, FOX_HEAD_DIM), jnp.float32)
    cache_v = nrm(ks[3], (DEPTH, n_pool, PAGE_SIZE, FOX_HEADS, FOX_HEAD_DIM), jnp.float32)
    cache_logf = jax.nn.log_sigmoid(FOX_FORGET_BIAS + nrm(ks[4], (DEPTH, n_pool, PAGE_SIZE, FOX_HEADS), jnp.float32))
    state_C = 0.1 * nrm(ks[5], (DEPTH, DEC_BATCH, MLSTM_HEADS, MLSTM_HEAD_DIM, MLSTM_HEAD_DIM), jnp.float32)
    state_n = 0.1 * nrm(ks[6], (DEPTH, DEC_BATCH, MLSTM_HEADS, MLSTM_HEAD_DIM), jnp.float32)
    state_m = nrm(ks[7], (DEPTH, DEC_BATCH, MLSTM_HEADS), jnp.float32)
    state_conv = nrm(ks[8], (DEPTH, DEC_BATCH, CONV_WIDTH - 1, 2 * MLSTM_WIDTH), jnp.float32)
    page_table = jax.random.permutation(ks[9], n_pool)[:n_used].reshape(DEC_BATCH, n_pages).astype(jnp.int32)
    col_scale = jnp.ones((IN_COLS,), jnp.float32).at[OFF_FOX_V:OFF_FOX_F].set(DN_BETA).at[OFF_M_V:OFF_M_I].set(DN_BETA)
    w_in = nrm(ks[10], (DEPTH, D_MODEL, IN_COLS), jnp.float32) * (D_MODEL ** -0.5) * col_scale
    b_fox_f = FOX_FORGET_BIAS + 0.1 * nrm(ks[11], (DEPTH, FOX_HEADS), jnp.float32)
    b_ig = 0.1 * nrm(ks[12], (DEPTH, MLSTM_HEADS), jnp.float32)
    b_fg = jnp.linspace(3.0, 6.0, MLSTM_HEADS, dtype=jnp.float32) + 0.1 * nrm(ks[13], (DEPTH, MLSTM_HEADS), jnp.float32)
    b_og = 0.02 * nrm(ks[14], (DEPTH, MLSTM_WIDTH), jnp.float32)
    conv_w = nrm(ks[15], (DEPTH, CONV_WIDTH, 2 * MLSTM_WIDTH), jnp.float32) * (CONV_WIDTH ** -0.5)
    conv_b = 0.02 * nrm(ks[16], (DEPTH, 2 * MLSTM_WIDTH), jnp.float32)
    mlstm_norm_w = 1.0 + 0.02 * nrm(ks[17], (DEPTH, MLSTM_WIDTH), jnp.float32)
    w_o = nrm(ks[18], (DEPTH, MIX_WIDTH, D_MODEL), jnp.float32) * (MIX_WIDTH ** -0.5) * DN_BETA
    ln1_g = 1.0 + 0.02 * nrm(ks[19], (DEPTH, D_MODEL), jnp.float32)
    ln1_b = 0.02 * nrm(ks[20], (DEPTH, D_MODEL), jnp.float32)
    w1 = nrm(ks[21], (DEPTH, D_MODEL, D_FF), jnp.float32) * (D_MODEL ** -0.5)
    w2 = nrm(ks[22], (DEPTH, D_FF, D_MODEL), jnp.float32) * (D_FF ** -0.5) * DN_BETA
    ln2_g = 1.0 + 0.02 * nrm(ks[23], (DEPTH, D_MODEL), jnp.float32)
    ln2_b = 0.02 * nrm(ks[24], (DEPTH, D_MODEL), jnp.float32)
    return {'x_prompt': x_prompt, 'x_sample': x_sample, 'cache_k': cache_k, 'cache_v': cache_v,
            'cache_logf': cache_logf, 'state_C': state_C, 'state_n': state_n, 'state_m': state_m,
            'state_conv': state_conv, 'page_table': page_table, 'w_in': w_in, 'b_fox_f': b_fox_f,
            'b_ig': b_ig, 'b_fg': b_fg, 'b_og': b_og, 'conv_w': conv_w, 'conv_b': conv_b,
            'mlstm_norm_w': mlstm_norm_w, 'w_o': w_o, 'ln1_g': ln1_g, 'ln1_b': ln1_b, 'w1': w1, 'w2': w2,
            'ln2_g': ln2_g, 'ln2_b': ln2_b}


def reference(x_prompt, x_sample, cache_k, cache_v, cache_logf, state_C, state_n, state_m, state_conv,
              page_table, w_in, b_fox_f, b_ig, b_fg, b_og, conv_w, conv_b, mlstm_norm_w, w_o,
              ln1_g, ln1_b, w1, w2, ln2_g, ln2_b):
    xp, xs = x_prompt, x_sample
    B = xp.shape[0]
    DB = xs.shape[0]
    n_pages = page_table.shape[1]
    past = n_pages * PAGE_SIZE
    kp_l, vp_l, lfp_l, Cp_l, np_l, mp_l, cp_l = [], [], [], [], [], [], []
    ks_l, vs_l, lfs_l, Cs_l, ns_l, ms_l, cs_l = [], [], [], [], [], [], []
    for l in range(DEPTH):
        gates = (w_in[l], b_fox_f[l], b_ig[l], b_fg[l], b_og[l], conv_w[l], conv_b[l])
        outw = (mlstm_norm_w[l], w_o[l], ln1_g[l], ln1_b[l], w1[l], w2[l], ln2_g[l], ln2_b[l])
        conv0 = jnp.zeros((B, CONV_WIDTH - 1, 2 * MLSTM_WIDTH), xp.dtype)
        fq, fk, fv, flogf, mq, mk, mv, ig, lf, og, conv_new = _mixer_inputs(xp, conv0, *gates)
        fox_h = _fox_prompt(fq, fk, fv, flogf)
        (C_p, n_p, m_p), m_h = _mlstm_prompt(mq, mk, mv, ig, lf)
        xp = _block_out(xp, fox_h, m_h, og, *outw)
        kp_l.append(fk); vp_l.append(fv); lfp_l.append(flogf)
        Cp_l.append(C_p); np_l.append(n_p); mp_l.append(m_p); cp_l.append(conv_new)
        sq, sk, sv, slogf, smq, smk, smv, sig, slf, sog, sconv_new = _mixer_inputs(xs, state_conv[l], *gates)
        k_past = cache_k[l][page_table].reshape(DB, past, FOX_HEADS, FOX_HEAD_DIM)
        v_past = cache_v[l][page_table].reshape(DB, past, FOX_HEADS, FOX_HEAD_DIM)
        logf_past = cache_logf[l][page_table].reshape(DB, past, FOX_HEADS)
        sfox_h = _fox_sample(sq, sk, sv, slogf, k_past, v_past, logf_past)
        carry0 = (state_C[l].astype(jnp.float32), state_n[l].astype(jnp.float32), state_m[l].astype(jnp.float32))
        (C_s, n_s, m_s), sm_h = _mlstm_chunk(carry0, (smq, smk, smv, sig, slf))
        xs = _block_out(xs, sfox_h, sm_h, sog, *outw)
        ks_l.append(sk); vs_l.append(sv); lfs_l.append(slogf)
        Cs_l.append(C_s); ns_l.append(n_s); ms_l.append(m_s); cs_l.append(sconv_new)
    y_prompt = xp
    y_sample = xs
    k_prompt = jnp.stack(kp_l); v_prompt = jnp.stack(vp_l); logf_prompt = jnp.stack(lfp_l)
    C_prompt = jnp.stack(Cp_l); n_prompt = jnp.stack(np_l); m_prompt = jnp.stack(mp_l); conv_prompt = jnp.stack(cp_l)
    k_sample = jnp.stack(ks_l); v_sample = jnp.stack(vs_l); logf_sample = jnp.stack(lfs_l)
    C_sample = jnp.stack(Cs_l); n_sample = jnp.stack(ns_l); m_sample = jnp.stack(ms_l); conv_sample = jnp.stack(cs_l)
    return (y_prompt, y_sample, k_prompt, v_prompt, logf_prompt, C_prompt, n_prompt, m_prompt, conv_prompt,
            k_sample, v_sample, logf_sample, C_sample, n_sample, m_sample, conv_sample)
```

```python
import functools

import jax
import jax.numpy as jnp
from jax import lax
from jax.experimental import pallas as pl
from jax.experimental.pallas import tpu as pltpu

F32 = jnp.float32
BF16 = jnp.bfloat16

V7X_VMEM_BYTES = 64 * 1024 * 1024
V7X_LANES = 128
V7X_SUBLANES = 8
LN_EPS = 1e-5
GATE_COLS = V7X_LANES
MLSTM_CHUNK = 128
PAD_IGATE = -1e30


def _params(semantics, vmem_bytes):
    return pltpu.CompilerParams(dimension_semantics=semantics,
                                vmem_limit_bytes=int(min(vmem_bytes, V7X_VMEM_BYTES - (4 << 20))))


def _const_spec(shape):
    zeros = (0,) * len(shape)
    return pl.BlockSpec(shape, lambda *_: zeros, pipeline_mode=pl.Buffered(1))


def _log_sigmoid(z):
    return jnp.minimum(z, 0.0) - jnp.log1p(jnp.exp(-jnp.abs(z)))


def _split3(a):
    hi = a.astype(BF16)
    r1 = a - hi.astype(F32)
    mid = r1.astype(BF16)
    lo = (r1 - mid.astype(F32)).astype(BF16)
    return hi, mid, lo


def _dot3_lhs(a, b_bf16):
    return sum(jnp.dot(p, b_bf16, preferred_element_type=F32) for p in _split3(a))


def _dot3_rhs(a_bf16, b):
    return sum(jnp.dot(a_bf16, p, preferred_element_type=F32) for p in _split3(b))


def _inproj_body(x_ref, w_ref, bg_ref, bog_ref, cw_ref, cb_ref, cprev_ref,
                 q_ref, kf_ref, kb_ref, vf_ref, vb_ref, g_ref, mq_ref, mk_ref, mv_ref, og_ref, cnew_ref,
                 zbuf_ref, *, fw, mw, n_fox, n_ml, fox_scale, mk_scale, conv_width):
    nb, T, D = x_ref.shape
    s = pl.program_id(1)
    x = x_ref[...].reshape(nb * T, D).astype(BF16)

    def proj(lo, width):
        return jnp.dot(x, w_ref[:, lo:lo + width], preferred_element_type=F32)

    off_k, off_v, off_qk = fw, 2 * fw, 3 * fw
    off_mv = off_qk + 2 * mw
    off_o = off_mv + mw
    off_g = off_o + mw

    q_ref[...] = (proj(0, fw) * fox_scale).astype(BF16).reshape(nb, T, fw)
    zk = proj(off_k, fw)
    kf_ref[...] = zk.reshape(nb, T, fw)
    kb_ref[...] = zk.astype(BF16).reshape(nb, T, fw)
    zv = proj(off_v, fw)
    vf_ref[...] = zv.reshape(nb, T, fw)
    vb_ref[...] = zv.astype(BF16).reshape(nb, T, fw)
    mv_ref[...] = proj(off_mv, mw).astype(BF16).reshape(nb, T, mw)
    og_ref[...] = jax.nn.sigmoid(proj(off_o, mw) + bog_ref[...]).astype(BF16).reshape(nb, T, mw)

    zg = proj(off_g, GATE_COLS) + bg_ref[...]
    col = lax.broadcasted_iota(jnp.int32, (1, GATE_COLS), 1)
    is_igate = (col >= n_fox) & (col < n_fox + n_ml)
    g_ref[...] = jnp.where(is_igate, zg, _log_sigmoid(zg)).reshape(nb, T, GATE_COLS)

    zqk = proj(off_qk, 2 * mw).reshape(nb, T, 2 * mw)
    hist = V7X_SUBLANES

    @pl.when(s == 0)
    def _():
        zbuf_ref[:, 0:hist, :] = cprev_ref[...]

    zbuf_ref[:, hist:hist + T, :] = zqk
    conv = cb_ref[...] + zqk * cw_ref[conv_width - 1:conv_width, :]
    for j in range(conv_width - 1):
        start = hist - (conv_width - 1) + j
        conv = conv + zbuf_ref[:, start:start + T, :] * cw_ref[j:j + 1, :]
    qk = conv * jax.nn.sigmoid(conv)
    mq_ref[...] = qk[:, :, :mw].astype(BF16)
    mk_ref[...] = (qk[:, :, mw:] * mk_scale).astype(BF16)
    tail = zbuf_ref[:, T:T + hist, :]
    cnew_ref[...] = tail
    zbuf_ref[:, 0:hist, :] = tail


def _inproj(x, w_all, b_gate, b_og, conv_w, conv_b, conv_prev8, *, fw, mw, n_fox, n_ml, dh, mdh, nb, T):
    B, S, D = x.shape
    ncols = w_all.shape[1]
    conv_width = conv_w.shape[0]
    assert B % nb == 0 and S % T == 0 and T % V7X_SUBLANES == 0 and T >= V7X_SUBLANES
    grid = (B // nb, S // T)
    tile = lambda width: pl.BlockSpec((nb, T, width), lambda b, s: (b, s, 0))
    per_seq = pl.BlockSpec((nb, V7X_SUBLANES, 2 * mw), lambda b, s: (b, 0, 0))
    out_shapes = [
        jax.ShapeDtypeStruct((B, S, fw), BF16),
        jax.ShapeDtypeStruct((B, S, fw), F32),
        jax.ShapeDtypeStruct((B, S, fw), BF16),
        jax.ShapeDtypeStruct((B, S, fw), F32),
        jax.ShapeDtypeStruct((B, S, fw), BF16),
        jax.ShapeDtypeStruct((B, S, GATE_COLS), F32),
        jax.ShapeDtypeStruct((B, S, mw), BF16),
        jax.ShapeDtypeStruct((B, S, mw), BF16),
        jax.ShapeDtypeStruct((B, S, mw), BF16),
        jax.ShapeDtypeStruct((B, S, mw), BF16),
        jax.ShapeDtypeStruct((B, V7X_SUBLANES, 2 * mw), F32),
    ]
    out_specs = [tile(fw), tile(fw), tile(fw), tile(fw), tile(fw), tile(GATE_COLS),
                 tile(mw), tile(mw), tile(mw), tile(mw), per_seq]
    rows = nb * T
    vmem = (2 * rows * D * 4 + D * ncols * 2 + 2 * rows * (5 * fw * 2 + 2 * fw * 4 + GATE_COLS * 4 + 4 * mw * 2) // 1
            + nb * (T + 8) * 2 * mw * 4 + 6 * rows * 2 * mw * 4 + (8 << 20))
    body = functools.partial(_inproj_body, fw=fw, mw=mw, n_fox=n_fox, n_ml=n_ml,
                             fox_scale=dh ** -0.5, mk_scale=mdh ** -0.5, conv_width=conv_width)
    return pl.pallas_call(
        body, grid=grid, out_shape=out_shapes,
        in_specs=[tile(D), _const_spec((D, ncols)), _const_spec((1, GATE_COLS)), _const_spec((1, mw)),
                  _const_spec(conv_w.shape), _const_spec((1, 2 * mw)), per_seq],
        out_specs=out_specs,
        scratch_shapes=[pltpu.VMEM((nb, T + V7X_SUBLANES, 2 * mw), F32)],
        compiler_params=_params(("arbitrary", "arbitrary"), vmem),
        name="inproj",
    )(x, w_all, b_gate, b_og, conv_w, conv_b, conv_prev8)


def _cumsum_body(x_ref, o_ref):
    R, S = x_ref.shape[1], x_ref.shape[2]
    L = V7X_LANES
    upper = (lax.broadcasted_iota(jnp.int32, (L, L), 0) <= lax.broadcasted_iota(jnp.int32, (L, L), 1)).astype(BF16)
    carry = jnp.zeros((R, 1), F32)
    for c in range(S // L):
        cs = _dot3_lhs(x_ref[0, :, c * L:(c + 1) * L], upper) + carry
        o_ref[0, :, c * L:(c + 1) * L] = cs
        carry = cs[:, L - 1:L]


def _cumsum_rows(x):
    B, R, S = x.shape
    assert S % V7X_LANES == 0
    spec = pl.BlockSpec((1, R, S), lambda b: (b, 0, 0))
    return pl.pallas_call(
        _cumsum_body, grid=(B,), out_shape=jax.ShapeDtypeStruct((B, R, S), F32),
        in_specs=[spec], out_specs=spec,
        compiler_params=_params(("arbitrary",), 8 * R * S * 4 + (8 << 20)),
        name="cumsum_rows",
    )(x)


def _fox_prompt_body(q_ref, k_ref, v_ref, l_ref, o_ref, m_sc, l_sc, acc_sc, *, tq, dh):
    qi = pl.program_id(2)
    q = q_ref[0]
    lane = lax.broadcasted_iota(jnp.int32, (1, 2 * dh), 1)
    zero = jnp.zeros_like(q)
    qs = [jnp.where((lane >= a * dh) & (lane < (a + 1) * dh), q, zero) for a in range(2)]
    m_sc[...] = jnp.full(m_sc.shape, -jnp.inf, F32)
    l_sc[...] = jnp.zeros(l_sc.shape, F32)
    acc_sc[...] = jnp.zeros(acc_sc.shape, F32)
    row = lax.broadcasted_iota(jnp.int32, (tq, tq), 0)
    colm = lax.broadcasted_iota(jnp.int32, (tq, tq), 1)

    def block(ki, masked):
        off = pl.multiple_of(ki * tq, tq)
        k = k_ref[0, pl.ds(off, tq), :]
        v = v_ref[0, pl.ds(off, tq), :]
        lk = l_ref[0, 0, :, pl.ds(off, tq)]
        for a in range(2):
            s = lax.dot_general(qs[a], k, (((1,), (1,)), ((), ())), preferred_element_type=F32)
            t = s - lk[a:a + 1, :]
            if masked:
                t = jnp.where(colm <= row, t, -jnp.inf)
            m_prev = m_sc[a]
            m_new = jnp.maximum(m_prev, jnp.max(t, axis=1, keepdims=True))
            alpha = jnp.exp(m_prev - m_new)
            p = jnp.exp(t - m_new)
            l_sc[a] = alpha * l_sc[a] + jnp.sum(p, axis=1, keepdims=True)
            acc_sc[a] = alpha * acc_sc[a] + jnp.dot(p.astype(BF16), v, preferred_element_type=F32)
            m_sc[a] = m_new

    def loop_body(ki, carry):
        block(ki, False)
        return carry

    lax.fori_loop(0, qi, loop_body, 0)
    block(qi, True)
    o = jnp.where(lane < dh, acc_sc[0] / l_sc[0], acc_sc[1] / l_sc[1])
    o_ref[0] = o.astype(o_ref.dtype)


def _fox_prompt(q, k, v, lcum, *, dh, tq):
    B, S, fw = q.shape
    pair = 2 * dh
    assert pair == V7X_LANES and fw % pair == 0 and S % tq == 0
    grid = (B, fw // pair, S // tq)
    kv_spec = pl.BlockSpec((1, S, pair), lambda b, hp, qi: (b, 0, hp))
    q_spec = pl.BlockSpec((1, tq, pair), lambda b, hp, qi: (b, qi, hp))
    vmem = 2 * 2 * S * pair * 2 + 4 * tq * pair * 2 + 2 * 2 * S * 4 * 8 + 4 * tq * V7X_LANES * 4 * 2 + 8 * tq * tq * 4 + (8 << 20)
    return pl.pallas_call(
        functools.partial(_fox_prompt_body, tq=tq, dh=dh), grid=grid,
        out_shape=jax.ShapeDtypeStruct((B, S, fw), BF16),
        in_specs=[q_spec, kv_spec, kv_spec, pl.BlockSpec((1, 1, 2, S), lambda b, hp, qi: (b, hp, 0, 0))],
        out_specs=q_spec,
        scratch_shapes=[pltpu.VMEM((2, tq, 1), F32), pltpu.VMEM((2, tq, 1), F32), pltpu.VMEM((2, tq, pair), F32)],
        compiler_params=_params(("arbitrary", "arbitrary", "arbitrary"), vmem),
        name="fox_prompt",
    )(q, k, v, lcum)


def _fox_sample_body(pt_ref, qs_ref, kn_ref, vn_ref, lfn_ref, k_ref, v_ref, lft_ref, tx_ref,
                     o_ref, m_sc, l_sc, acc_sc, tail_sc, *, n_heads, n_new, page, dh):
    del pt_ref
    p = pl.program_id(1)
    H, T = n_heads, n_new
    q = qs_ref[0]

    def update(s2, bias3, v_flat, first):
        width = s2.shape[1]
        t3 = s2.reshape(H, T, width) + bias3
        m_blk = jnp.max(t3, axis=2, keepdims=True)
        if first:
            m_new = m_blk
        else:
            m_prev = m_sc[...]
            m_new = jnp.maximum(m_prev, m_blk)
            alpha = jnp.exp(m_prev - m_new)
        pm = jnp.exp(t3 - m_new)
        l_blk = jnp.sum(pm, axis=2, keepdims=True)
        pv = jnp.dot(pm.reshape(H * T, width).astype(BF16), v_flat, preferred_element_type=F32).reshape(H, T, dh)
        if first:
            l_sc[...] = l_blk
            acc_sc[...] = pv
        else:
            l_sc[...] = alpha * l_sc[...] + l_blk
            acc_sc[...] = alpha * acc_sc[...] + pv
        m_sc[...] = m_new

    @pl.when(p == 0)
    def _():
        width = T * H
        s_new = lax.dot_general(q, kn_ref[0], (((1,), (1,)), ((), ())), preferred_element_type=F32)
        lfn = lfn_ref[0]
        jpos = lax.broadcasted_iota(jnp.int32, (1, width), 1) // H
        ln = jnp.zeros((H, width), F32)
        for i in range(T):
            ln = ln + jnp.where(jpos >= i, lfn[:, i:i + 1], 0.0)
        r3 = lax.broadcasted_iota(jnp.int32, (H, T, width), 2)
        h3 = lax.broadcasted_iota(jnp.int32, (H, T, width), 0)
        t3i = lax.broadcasted_iota(jnp.int32, (H, T, width), 1)
        ok = (r3 % H == h3) & (r3 // H <= t3i)
        bias = jnp.where(ok, -ln[:, None, :], -jnp.inf)
        update(s_new, bias, vn_ref[0], True)
        tail_sc[...] = jnp.zeros(tail_sc.shape, F32)

    rows = page * H
    kf = k_ref[0].reshape(rows, dh).astype(BF16)
    vf = v_ref[0].reshape(rows, dh).astype(BF16)
    s_past = lax.dot_general(q, kf, (((1,), (1,)), ((), ())), preferred_element_type=F32)
    lft = lft_ref[0]
    within = _dot3_lhs(lft, tx_ref[...])
    tail = tail_sc[...]
    rr = lax.broadcasted_iota(jnp.int32, (H, rows), 1)
    hh = lax.broadcasted_iota(jnp.int32, (H, rows), 0)
    bm = jnp.where(rr % H == hh, within + tail, -jnp.inf)
    update(s_past, bm[:, None, :], vf, False)
    tail_sc[...] = tail + jnp.sum(lft, axis=1, keepdims=True)

    @pl.when(p == pl.num_programs(1) - 1)
    def _():
        o_ref[0] = (acc_sc[...] / l_sc[...]).reshape(H * T, dh)


def _fox_sample(page_table, qs, kn, vn, lfn_t, cache_k, cache_v, lft_pool, *, n_heads, n_new):
    DB, n_pages = page_table.shape
    _, page, H, dh = cache_k.shape
    T = n_new
    rows = page * H
    tx = (lax.broadcasted_iota(jnp.int32, (page, rows), 0) > lax.broadcasted_iota(jnp.int32, (page, rows), 1) // H).astype(BF16)
    pt_flat = page_table.reshape(-1).astype(jnp.int32)

    def page_idx(b, p, pt):
        return pt[b * n_pages + (n_pages - 1 - p)]

    per_b = lambda shape: pl.BlockSpec((1,) + shape, lambda b, p, pt: (b, 0, 0))
    grid_spec = pltpu.PrefetchScalarGridSpec(
        num_scalar_prefetch=1, grid=(DB, n_pages),
        in_specs=[per_b((H * T, dh)), per_b((T * H, dh)), per_b((T * H, dh)), per_b((H, T)),
                  pl.BlockSpec((1, page, H, dh), lambda b, p, pt: (page_idx(b, p, pt), 0, 0, 0)),
                  pl.BlockSpec((1, page, H, dh), lambda b, p, pt: (page_idx(b, p, pt), 0, 0, 0)),
                  pl.BlockSpec((1, H, page), lambda b, p, pt: (page_idx(b, p, pt), 0, 0)),
                  pl.BlockSpec((page, rows), lambda b, p, pt: (0, 0))],
        out_specs=per_b((H * T, dh)),
        scratch_shapes=[pltpu.VMEM((H, T, 1), F32), pltpu.VMEM((H, T, 1), F32), pltpu.VMEM((H, T, dh), F32),
                        pltpu.VMEM((H, 1), F32)])
    vmem = 2 * 2 * rows * V7X_LANES * 4 + 2 * page * rows * 2 + 16 * H * T * rows * 4 + (8 << 20)
    return pl.pallas_call(
        functools.partial(_fox_sample_body, n_heads=H, n_new=T, page=page, dh=dh),
        grid_spec=grid_spec, out_shape=jax.ShapeDtypeStruct((DB, H * T, dh), F32),
        compiler_params=_params(("arbitrary", "arbitrary"), vmem),
        name="fox_sample",
    )(pt_flat, qs, kn, vn, lfn_t, cache_k, cache_v, lft_pool, tx)


def _mlstm_body(q_ref, k_ref, v_ref, og_ref, g_ref, nw_ref, c0_ref, n0_ref, m0_ref,
                mn_ref, c_ref, n_ref, m_ref, *, n_heads, mdh, ig_col, lf_col):
    c = pl.program_id(1)
    L = q_ref.shape[1]

    @pl.when(c == 0)
    def _():
        c_ref[...] = c0_ref[...]
        n_ref[...] = n0_ref[...]
        m_ref[...] = m0_ref[...]

    G = g_ref[0]
    ri = lax.broadcasted_iota(jnp.int32, (L, L), 0)
    ci = lax.broadcasted_iota(jnp.int32, (L, L), 1)
    causal = ci <= ri
    Bc = _dot3_rhs(causal.astype(BF16), G)
    GT = G.T
    BcT = Bc.T
    for h in range(n_heads):
        sl = slice(h * mdh, (h + 1) * mdh)
        q = q_ref[0, :, sl]
        k = k_ref[0, :, sl]
        v = v_ref[0, :, sl]
        b_col = Bc[:, lf_col + h:lf_col + h + 1]
        b_row = BcT[lf_col + h:lf_col + h + 1, :]
        ig_colv = G[:, ig_col + h:ig_col + h + 1]
        ig_row = GT[ig_col + h:ig_col + h + 1, :]
        m_prev = m_ref[0, h][:, 0:1]
        C = c_ref[0, h]
        n = n_ref[0, h]
        Dm = jnp.where(causal, (b_col - b_row) + ig_row, -jnp.inf)
        inter = m_prev + b_col
        m_t = jnp.maximum(inter, jnp.max(Dm, axis=1, keepdims=True))
        W = jnp.exp(Dm - m_t)
        a = jnp.exp(inter - m_t)
        s = lax.dot_general(q, k, (((1,), (1,)), ((), ())), preferred_element_type=F32)
        Wqk = W * s
        num = (jnp.dot(Wqk.astype(BF16), v, preferred_element_type=F32)
               + jnp.dot(q, C.astype(BF16), preferred_element_type=F32) * a)
        nq = jnp.sum(Wqk, axis=1, keepdims=True) + jnp.sum(q.astype(F32) * n, axis=1, keepdims=True) * a
        den = jnp.maximum(jnp.abs(nq), jnp.exp(-m_t))
        hh = num / den
        mu = jnp.mean(hh, axis=1, keepdims=True)
        d = hh - mu
        var = jnp.mean(d * d, axis=1, keepdims=True)
        mn = d * lax.rsqrt(var + LN_EPS) * nw_ref[:, sl] * og_ref[0, :, sl].astype(F32)
        mn_ref[0, :, sl] = mn.astype(mn_ref.dtype)
        m_end = m_t[L - 1:L, :]
        g_state = jnp.exp(inter[L - 1:L, :] - m_end)
        wk = jnp.exp((b_col[L - 1:L, :] - b_col) + ig_colv - m_end)
        kw = k.astype(F32) * wk
        c_ref[0, h] = g_state * C + jnp.dot(kw.T.astype(BF16), v, preferred_element_type=F32)
        n_ref[0, h] = g_state * n + jnp.sum(kw, axis=0, keepdims=True)
        m_ref[0, h] = jnp.broadcast_to(m_end, (1, V7X_LANES))


def _mlstm(q, k, v, og, gates, norm_w, c0, n0, m0, *, n_heads, ig_col, lf_col, chunk):
    B, S, mw = q.shape
    mdh = mw // n_heads
    assert S % chunk == 0 and mdh == V7X_LANES
    tile = lambda width: pl.BlockSpec((1, chunk, width), lambda b, c: (b, c, 0))
    st = lambda shape: pl.BlockSpec((1,) + shape, lambda b, c: (b, 0, 0, 0))
    c_s, n_s, m_s = st((n_heads, mdh, mdh)), st((n_heads, 1, mdh)), st((n_heads, 1, V7X_LANES))
    vmem = 2 * 5 * chunk * mw * 2 + 2 * chunk * GATE_COLS * 4 + 6 * n_heads * mdh * mdh * 4 + 64 * chunk * chunk * 4 + (8 << 20)
    return pl.pallas_call(
        functools.partial(_mlstm_body, n_heads=n_heads, mdh=mdh, ig_col=ig_col, lf_col=lf_col),
        grid=(B, S // chunk),
        out_shape=[jax.ShapeDtypeStruct((B, S, mw), BF16),
                   jax.ShapeDtypeStruct((B, n_heads, mdh, mdh), F32),
                   jax.ShapeDtypeStruct((B, n_heads, 1, mdh), F32),
                   jax.ShapeDtypeStruct((B, n_heads, 1, V7X_LANES), F32)],
        in_specs=[tile(mw), tile(mw), tile(mw), tile(mw), tile(GATE_COLS), _const_spec((1, mw)), c_s, n_s, m_s],
        out_specs=[tile(mw), c_s, n_s, m_s],
        compiler_params=_params(("arbitrary", "arbitrary"), vmem),
        name="mlstm",
    )(q, k, v, og, gates, norm_w, c0, n0, m0)


def _layer_norm(u, g, b):
    mu = jnp.mean(u, axis=-1, keepdims=True)
    d = u - mu
    var = jnp.mean(d * d, axis=-1, keepdims=True)
    return d * lax.rsqrt(var + LN_EPS) * g + b


def _block_out_body(x_ref, fox_ref, mn_ref, wo_ref, g1_ref, b1_ref, w1_ref, w2_ref, g2_ref, b2_ref, y_ref,
                    *, alpha, ff_chunk):
    fw = fox_ref.shape[1]
    x = x_ref[...]
    mix = (jnp.dot(fox_ref[...], wo_ref[0:fw, :], preferred_element_type=F32)
           + jnp.dot(mn_ref[...], wo_ref[fw:, :], preferred_element_type=F32))
    x1 = _layer_norm(alpha * x + mix, g1_ref[...], b1_ref[...])
    x1b = x1.astype(BF16)
    acc = jnp.zeros(x.shape, F32)
    for c in range(0, w1_ref.shape[1], ff_chunk):
        hid = jnp.dot(x1b, w1_ref[:, c:c + ff_chunk], preferred_element_type=F32)
        hid = jnp.square(jnp.maximum(hid, 0.0)).astype(BF16)
        acc = acc + jnp.dot(hid, w2_ref[c:c + ff_chunk, :], preferred_element_type=F32)
    y_ref[...] = _layer_norm(alpha * x1 + acc, g2_ref[...], b2_ref[...])


def _block_out(x, fox_h, mn, wo, g1, b1, w1, w2, g2, b2, *, alpha, tm):
    N, D = x.shape
    fw, mw, dff = fox_h.shape[1], mn.shape[1], w1.shape[1]
    assert N % tm == 0
    ff_chunk = min(dff, 1024)
    tile = lambda width: pl.BlockSpec((tm, width), lambda i: (i, 0))
    vec = _const_spec((1, D))
    vmem = (4 * tm * D * 4 + 4 * tm * (fw + mw) * 2 + (wo.size + w1.size + w2.size) * 2
            + 6 * tm * D * 4 + 3 * tm * ff_chunk * 4 + (8 << 20))
    return pl.pallas_call(
        functools.partial(_block_out_body, alpha=alpha, ff_chunk=ff_chunk), grid=(N // tm,),
        out_shape=jax.ShapeDtypeStruct((N, D), F32),
        in_specs=[tile(D), tile(fw), tile(mw), _const_spec(wo.shape), vec, vec,
                  _const_spec(w1.shape), _const_spec(w2.shape), vec, vec],
        out_specs=tile(D),
        compiler_params=_params(("arbitrary",), vmem),
        name="block_out",
    )(x, fox_h, mn, wo, g1, b1, w1, w2, g2, b2)


def _pad_rows(a, total, value=0.0):
    return jnp.pad(a, ((0, 0), (0, total - a.shape[1]), (0, 0)), constant_values=value)


def kernel(x_prompt, x_sample, cache_k, cache_v, cache_logf, state_C, state_n, state_m, state_conv, page_table,
           w_in, b_fox_f, b_ig, b_fg, b_og, conv_w, conv_b, mlstm_norm_w, w_o, ln1_g, ln1_b, w1, w2, ln2_g, ln2_b):
    depth = w_in.shape[0]
    B, S, D = x_prompt.shape
    DB, T, _ = x_sample.shape
    H, dh = cache_k.shape[3], cache_k.shape[4]
    MH = b_ig.shape[1]
    fw, mw = H * dh, b_og.shape[1]
    mdh = mw // MH
    cwid = conv_w.shape[1]
    alpha = (2 * depth) ** 0.25
    assert H + 2 * MH <= GATE_COLS and T >= cwid - 1 and S >= cwid - 1
    ig_col, lf_col = H, H + MH

    o_q, o_k, o_v, o_f = 0, fw, 2 * fw, 3 * fw
    o_mqk = o_f + H
    o_mv = o_mqk + 2 * mw
    o_mi = o_mv + mw
    o_mf = o_mi + MH
    o_mo = o_mf + MH
    assert w_in.shape[2] == o_mo + mw

    tile_s = min(S, 512)
    xp, xs = x_prompt, x_sample
    outs = [[] for _ in range(14)]
    for l in range(depth):
        wl = w_in[l]
        gate_w = jnp.concatenate([wl[:, o_f:o_mqk], wl[:, o_mi:o_mo],
                                  jnp.zeros((D, GATE_COLS - H - 2 * MH), wl.dtype)], axis=1)
        w_all = jnp.concatenate([wl[:, o_q:o_f], wl[:, o_mqk:o_mi], wl[:, o_mo:], gate_w], axis=1).astype(BF16)
        b_gate = jnp.concatenate([b_fox_f[l], b_ig[l], b_fg[l], jnp.zeros((GATE_COLS - H - 2 * MH,), F32)])[None, :]
        inproj = functools.partial(_inproj, w_all=w_all, b_gate=b_gate, b_og=b_og[l][None, :], conv_w=conv_w[l],
                                   conv_b=conv_b[l][None, :], fw=fw, mw=mw, n_fox=H, n_ml=MH, dh=dh, mdh=mdh)
        wo_b, w1_b, w2_b = w_o[l].astype(BF16), w1[l].astype(BF16), w2[l].astype(BF16)
        block_out = functools.partial(_block_out, wo=wo_b, g1=ln1_g[l][None, :], b1=ln1_b[l][None, :], w1=w1_b, w2=w2_b,
                                      g2=ln2_g[l][None, :], b2=ln2_b[l][None, :], alpha=alpha)
        mlstm = functools.partial(_mlstm, norm_w=mlstm_norm_w[l][None, :], n_heads=MH, ig_col=ig_col, lf_col=lf_col,
                                  chunk=MLSTM_CHUNK)
        hist = V7X_SUBLANES

        (q, kf, kb, vf, vb, gates, mq, mk, mv, og, conv_tail) = inproj(
            xp, conv_prev8=jnp.zeros((B, hist, 2 * mw), F32), nb=1, T=tile_s)
        lcum = _cumsum_rows(jnp.swapaxes(gates[:, :, :H], 1, 2))
        fox_h = _fox_prompt(q, kb, vb, lcum.reshape(B, H // 2, 2, S), dh=dh, tq=tile_s)
        mn, C_p, n_p, m_p = mlstm(mq, mk, mv, og, gates,
                                  c0=jnp.zeros((B, MH, mdh, mdh), F32), n0=jnp.zeros((B, MH, 1, mdh), F32),
                                  m0=jnp.zeros((B, MH, 1, V7X_LANES), F32))
        xp = block_out(xp.reshape(B * S, D), fox_h.reshape(B * S, fw), mn.reshape(B * S, mw), tm=tile_s).reshape(B, S, D)
        outs[0].append(kf.reshape(B, S, H, dh)); outs[1].append(vf.reshape(B, S, H, dh))
        outs[2].append(gates[:, :, :H]); outs[3].append(C_p); outs[4].append(n_p[:, :, 0, :])
        outs[5].append(m_p[:, :, 0, 0]); outs[6].append(conv_tail[:, hist - (cwid - 1):, :])

        prev8 = jnp.pad(state_conv[l], ((0, 0), (hist - (cwid - 1), 0), (0, 0)))
        (sq, skf, skb, svf, svb, sgates, smq, smk, smv, sog, sconv_tail) = inproj(xs, conv_prev8=prev8, nb=DB, T=T)
        qs = jnp.swapaxes(sq.reshape(DB, T, H, dh), 1, 2).reshape(DB, H * T, dh)
        lfn_t = jnp.swapaxes(sgates[:, :, :H], 1, 2)
        lft_pool = jnp.swapaxes(cache_logf[l], 1, 2)
        o_s = _fox_sample(page_table, qs, skb.reshape(DB, T * H, dh), svb.reshape(DB, T * H, dh), lfn_t,
                          cache_k[l], cache_v[l], lft_pool, n_heads=H, n_new=T)
        sfox_h = jnp.swapaxes(o_s.reshape(DB, H, T, dh), 1, 2).reshape(DB * T, fw).astype(BF16)
        pad_gate = jnp.zeros((GATE_COLS,), F32).at[ig_col:ig_col + MH].set(PAD_IGATE)
        sg_pad = jnp.concatenate([sgates, jnp.broadcast_to(pad_gate, (DB, MLSTM_CHUNK - T, GATE_COLS))], axis=1)
        pr = lambda a: _pad_rows(a, MLSTM_CHUNK)
        smn, C_s, n_s, m_s = mlstm(pr(smq), pr(smk), pr(smv), pr(sog), sg_pad,
                                   c0=state_C[l].astype(F32), n0=state_n[l].astype(F32)[:, :, None, :],
                                   m0=jnp.broadcast_to(state_m[l].astype(F32)[:, :, None, None], (DB, MH, 1, V7X_LANES)))
        xs = block_out(xs.reshape(DB * T, D), sfox_h, smn[:, :T, :].reshape(DB * T, mw), tm=DB * T).reshape(DB, T, D)
        outs[7].append(skf.reshape(DB, T, H, dh)); outs[8].append(svf.reshape(DB, T, H, dh))
        outs[9].append(sgates[:, :, :H]); outs[10].append(C_s); outs[11].append(n_s[:, :, 0, :])
        outs[12].append(m_s[:, :, 0, 0]); outs[13].append(sconv_tail[:, hist - (cwid - 1):, :])

    stacked = [jnp.stack(o) for o in outs]
    return (xp, xs, *stacked)
```

```python
import functools
import math

import jax
import jax.numpy as jnp
from jax import lax
from jax.experimental import pallas as pl
from jax.experimental.pallas import tpu as pltpu

F32 = jnp.float32
BF16 = jnp.bfloat16

V7X_VMEM_BYTES = 64 * 1024 * 1024
V7X_LANES = 128
V7X_SUBLANES = 8
LN_EPS = 1e-5
LOG2E = math.log2(math.e)
GATE_COLS = V7X_LANES
MLSTM_CHUNK = 128
PAD_IGATE = -1e30


def _params(semantics, vmem_bytes):
    return pltpu.CompilerParams(dimension_semantics=semantics,
                                vmem_limit_bytes=int(min(vmem_bytes, V7X_VMEM_BYTES - (4 << 20))))


def _const_spec(shape):
    zeros = (0,) * len(shape)
    return pl.BlockSpec(shape, lambda *_: zeros, pipeline_mode=pl.Buffered(1))


def _log_sigmoid(z):
    return jnp.minimum(z, 0.0) - jnp.log1p(jnp.exp(-jnp.abs(z)))


def _split3(a):
    hi = a.astype(BF16)
    r1 = a - hi.astype(F32)
    mid = r1.astype(BF16)
    lo = (r1 - mid.astype(F32)).astype(BF16)
    return hi, mid, lo


def _dot3_lhs(a, b_bf16):
    return sum(jnp.dot(p, b_bf16, preferred_element_type=F32) for p in _split3(a))


def _dot3_rhs(a_bf16, b):
    return sum(jnp.dot(a_bf16, p, preferred_element_type=F32) for p in _split3(b))


def _tri(n, cmp):
    i = lax.broadcasted_iota(jnp.int32, (n, n), 0)
    j = lax.broadcasted_iota(jnp.int32, (n, n), 1)
    return cmp(i, j).astype(BF16)


def _inproj_body(x_ref, w_ref, bg_ref, bog_ref, cw_ref, cb_ref, cprev_ref, *refs,
                 fw, mw, n_fox, n_ml, q_scale, mk_scale, conv_width, feature_major):
    if feature_major:
        (qT_ref, kb_ref, kT_ref, vT_ref, vTb_ref, g_ref, mq_ref, mk_ref, mv_ref, og_ref, cnew_ref, zbuf_ref) = refs
    else:
        (q_ref, kf_ref, vf_ref, g_ref, mq_ref, mk_ref, mv_ref, og_ref, cnew_ref, zbuf_ref) = refs
    nb, T, D = x_ref.shape
    s = pl.program_id(1)
    x = x_ref[...].reshape(nb * T, D).astype(BF16)

    def proj(lo, width):
        return jnp.dot(x, w_ref[:, lo:lo + width], preferred_element_type=F32)

    off_k, off_v, off_qk = fw, 2 * fw, 3 * fw
    off_mv = off_qk + 2 * mw
    off_o = off_mv + mw
    off_g = off_o + mw

    zq = proj(0, fw) * q_scale
    zk = proj(off_k, fw)
    zv = proj(off_v, fw)
    if feature_major:
        qT_ref[0] = zq.T.astype(BF16)
        kb_ref[...] = zk.astype(BF16).reshape(nb, T, fw)
        kT_ref[0] = zk.T
        vT = zv.T
        vT_ref[0] = vT
        vTb_ref[0] = vT.astype(BF16)
    else:
        q_ref[...] = zq.astype(BF16).reshape(nb, T, fw)
        kf_ref[...] = zk.reshape(nb, T, fw)
        vf_ref[...] = zv.reshape(nb, T, fw)
    mv_ref[...] = proj(off_mv, mw).astype(BF16).reshape(nb, T, mw)
    og_ref[...] = jax.nn.sigmoid(proj(off_o, mw) + bog_ref[...]).astype(BF16).reshape(nb, T, mw)

    zg = proj(off_g, GATE_COLS) + bg_ref[...]
    col = lax.broadcasted_iota(jnp.int32, (1, GATE_COLS), 1)
    is_igate = (col >= n_fox) & (col < n_fox + n_ml)
    g_ref[...] = jnp.where(is_igate, zg, _log_sigmoid(zg)).reshape(nb, T, GATE_COLS)

    zqk = proj(off_qk, 2 * mw).reshape(nb, T, 2 * mw)
    hist = V7X_SUBLANES

    @pl.when(s == 0)
    def _():
        zbuf_ref[:, 0:hist, :] = cprev_ref[...]

    zbuf_ref[:, hist:hist + T, :] = zqk
    conv = cb_ref[...] + zqk * cw_ref[conv_width - 1:conv_width, :]
    for j in range(conv_width - 1):
        start = hist - (conv_width - 1) + j
        conv = conv + zbuf_ref[:, start:start + T, :] * cw_ref[j:j + 1, :]
    qk = conv * jax.nn.sigmoid(conv)
    mq_ref[...] = qk[:, :, :mw].astype(BF16)
    mk_ref[...] = (qk[:, :, mw:] * mk_scale).astype(BF16)
    tail = zbuf_ref[:, T:T + hist, :]
    cnew_ref[...] = tail
    zbuf_ref[:, 0:hist, :] = tail


def _inproj(x, w_all, b_gate, b_og, conv_w, conv_b, conv_prev8, *, fw, mw, n_fox, n_ml, dh, mdh, nb, T,
            feature_major):
    B, S, D = x.shape
    ncols = w_all.shape[1]
    conv_width = conv_w.shape[0]
    assert B % nb == 0 and S % T == 0 and T % V7X_SUBLANES == 0 and T >= V7X_SUBLANES
    grid = (B // nb, S // T)
    tile = lambda width: pl.BlockSpec((nb, T, width), lambda b, s: (b, s, 0))
    per_seq = pl.BlockSpec((nb, V7X_SUBLANES, 2 * mw), lambda b, s: (b, 0, 0))
    tok = lambda width, dt: jax.ShapeDtypeStruct((B, S, width), dt)
    if feature_major:
        assert nb == 1 and T % V7X_LANES == 0
        feat = lambda dt: jax.ShapeDtypeStruct((B, fw, S), dt)
        feat_spec = pl.BlockSpec((1, fw, T), lambda b, s: (b, 0, s))
        fox_shapes = [feat(BF16), tok(fw, BF16), feat(F32), feat(F32), feat(BF16)]
        fox_specs = [feat_spec, tile(fw), feat_spec, feat_spec, feat_spec]
    else:
        fox_shapes = [tok(fw, BF16), tok(fw, F32), tok(fw, F32)]
        fox_specs = [tile(fw)] * 3
    out_shapes = fox_shapes + [tok(GATE_COLS, F32), tok(mw, BF16), tok(mw, BF16), tok(mw, BF16), tok(mw, BF16),
                               jax.ShapeDtypeStruct((B, V7X_SUBLANES, 2 * mw), F32)]
    out_specs = fox_specs + [tile(GATE_COLS), tile(mw), tile(mw), tile(mw), tile(mw), per_seq]
    rows = nb * T
    vmem = (2 * rows * D * 4 + D * ncols * 2 + 2 * rows * (16 * fw + GATE_COLS * 4 + 4 * mw * 2)
            + nb * (T + 8) * 2 * mw * 4 + 6 * rows * 2 * mw * 4 + (8 << 20))
    body = functools.partial(_inproj_body, fw=fw, mw=mw, n_fox=n_fox, n_ml=n_ml,
                             q_scale=dh ** -0.5 * LOG2E, mk_scale=mdh ** -0.5, conv_width=conv_width,
                             feature_major=feature_major)
    return pl.pallas_call(
        body, grid=grid, out_shape=out_shapes,
        in_specs=[tile(D), _const_spec((D, ncols)), _const_spec((1, GATE_COLS)), _const_spec((1, mw)),
                  _const_spec(conv_w.shape), _const_spec((1, 2 * mw)), per_seq],
        out_specs=out_specs,
        scratch_shapes=[pltpu.VMEM((nb, T + V7X_SUBLANES, 2 * mw), F32)],
        compiler_params=_params(("arbitrary", "arbitrary"), vmem),
        name="inproj",
    )(x, w_all, b_gate, b_og, conv_w, conv_b, conv_prev8)


def _cumsum_body(x_ref, o_ref):
    R, S = x_ref.shape[1], x_ref.shape[2]
    L = V7X_LANES
    upper = _tri(L, lambda i, j: i <= j)
    carry = jnp.zeros((R, 1), F32)
    for c in range(S // L):
        cs = _dot3_lhs(x_ref[0, :, c * L:(c + 1) * L], upper) + carry
        o_ref[0, :, c * L:(c + 1) * L] = cs
        carry = cs[:, L - 1:L]


def _cumsum_rows(x):
    B, R, S = x.shape
    assert S % V7X_LANES == 0
    spec = pl.BlockSpec((1, R, S), lambda b: (b, 0, 0))
    return pl.pallas_call(
        _cumsum_body, grid=(B,), out_shape=jax.ShapeDtypeStruct((B, R, S), F32),
        in_specs=[spec], out_specs=spec,
        compiler_params=_params(("arbitrary",), 8 * R * S * 4 + (8 << 20)),
        name="cumsum_rows",
    )(x)


def _fox_prompt_body(qT_ref, k_ref, vT_ref, l_ref, o_ref, lcol_sc, m_sc, l_sc, acc_sc, *, tq, dh):
    qi = pl.program_id(2)
    S = k_ref.shape[1]
    L = V7X_LANES

    @pl.when(qi == 0)
    def _():
        for a in range(2):
            for c in range(S // L):
                row = l_ref[0, 0, a:a + 1, c * L:(c + 1) * L] * LOG2E
                lcol_sc[a, c * L:(c + 1) * L, :] = jnp.broadcast_to(row, (L, L)).T

    qT = qT_ref[0]
    frow = lax.broadcasted_iota(jnp.int32, (2 * dh, 1), 0)
    zero = jnp.zeros_like(qT)
    qTs = [jnp.where((frow >= a * dh) & (frow < (a + 1) * dh), qT, zero) for a in range(2)]
    m_sc[...] = jnp.full(m_sc.shape, -jnp.inf, F32)
    l_sc[...] = jnp.zeros(l_sc.shape, F32)
    acc_sc[...] = jnp.zeros(acc_sc.shape, F32)
    key_i = lax.broadcasted_iota(jnp.int32, (tq, tq), 0)
    qry_j = lax.broadcasted_iota(jnp.int32, (tq, tq), 1)

    def block(ki, masked):
        off = pl.multiple_of(ki * tq, tq)
        k = k_ref[0, pl.ds(off, tq), :]
        vT = vT_ref[0, :, pl.ds(off, tq)]
        for a in range(2):
            sT = jnp.dot(k, qTs[a], preferred_element_type=F32)
            lc = lcol_sc[a, pl.ds(off, tq), :]
            t = sT - jnp.concatenate([lc] * (tq // L), axis=1)
            if masked:
                t = jnp.where(key_i <= qry_j, t, -jnp.inf)
            m_prev = m_sc[a]
            m_new = jnp.maximum(m_prev, jnp.max(t, axis=0, keepdims=True))
            alpha = jnp.exp2(m_prev - m_new)
            p = jnp.exp2(t - m_new)
            l_sc[a] = alpha * l_sc[a] + jnp.sum(p, axis=0, keepdims=True)
            pv = jnp.dot(vT[a * dh:(a + 1) * dh, :], p.astype(BF16), preferred_element_type=F32)
            acc_sc[a] = alpha * acc_sc[a] + pv
            m_sc[a] = m_new

    def loop_body(ki, carry):
        block(ki, False)
        return carry

    lax.fori_loop(0, qi, loop_body, 0)
    block(qi, True)
    oT = jnp.concatenate([acc_sc[0] / l_sc[0], acc_sc[1] / l_sc[1]], axis=0)
    o_ref[0] = oT.T.astype(o_ref.dtype)


def _fox_prompt(qT, k, vT, lcum, *, dh, tq):
    B, S, fw = k.shape
    pair = 2 * dh
    assert pair == V7X_LANES and fw % pair == 0 and S % tq == 0 and tq % V7X_LANES == 0
    grid = (B, fw // pair, S // tq)
    vmem = (2 * 2 * S * pair * 2 + 4 * tq * pair * 2 + 2 * 2 * S * 4 * 8 + 2 * S * V7X_LANES * 4
            + 4 * pair * tq * 4 + 10 * tq * tq * 4 + (8 << 20))
    return pl.pallas_call(
        functools.partial(_fox_prompt_body, tq=tq, dh=dh), grid=grid,
        out_shape=jax.ShapeDtypeStruct((B, S, fw), BF16),
        in_specs=[pl.BlockSpec((1, pair, tq), lambda b, hp, qi: (b, hp, qi)),
                  pl.BlockSpec((1, S, pair), lambda b, hp, qi: (b, 0, hp)),
                  pl.BlockSpec((1, pair, S), lambda b, hp, qi: (b, hp, 0)),
                  pl.BlockSpec((1, 1, 2, S), lambda b, hp, qi: (b, hp, 0, 0))],
        out_specs=pl.BlockSpec((1, tq, pair), lambda b, hp, qi: (b, qi, hp)),
        scratch_shapes=[pltpu.VMEM((2, S, V7X_LANES), F32), pltpu.VMEM((2, 1, tq), F32), pltpu.VMEM((2, 1, tq), F32),
                        pltpu.VMEM((2, dh, tq), F32)],
        compiler_params=_params(("arbitrary", "arbitrary", "arbitrary"), vmem),
        name="fox_prompt",
    )(qT, k, vT, lcum)


def _fox_sample_body(pt_ref, q_ref, knT_ref, vnT_ref, lfn_ref, suf_ref, pre_ref, *refs,
                     n_heads, n_new, dh, group):
    del pt_ref
    k_refs, v_refs, lf_refs = refs[0:group], refs[group:2 * group], refs[2 * group:3 * group]
    o_ref, qbd_sc, m_sc, l_sc, acc_sc, tail_sc = refs[3 * group:]
    p = pl.program_id(1)
    H, T = n_heads, n_new
    fw = H * dh
    page = k_refs[0].shape[2]
    lane_head = lax.broadcasted_iota(jnp.int32, (1, fw), 1) // dh

    def update(kT, vT, bias3, first):
        s2 = jnp.dot(qbd_sc[...], kT.astype(BF16), preferred_element_type=F32)
        t3 = s2.reshape(H, T, page) + bias3
        m_blk = jnp.max(t3, axis=2, keepdims=True)
        if first:
            m_new = m_blk
        else:
            m_prev = m_sc[...]
            m_new = jnp.maximum(m_prev, m_blk)
            alpha = jnp.exp2(m_prev - m_new)
        pm = jnp.exp2(t3 - m_new)
        l_blk = jnp.sum(pm, axis=2, keepdims=True)
        pv = lax.dot_general(pm.reshape(H * T, page).astype(BF16), vT.astype(BF16), (((1,), (1,)), ((), ())),
                             preferred_element_type=F32).reshape(H, T, fw)
        if first:
            l_sc[...] = l_blk
            acc_sc[...] = pv
        else:
            l_sc[...] = alpha * l_sc[...] + l_blk
            acc_sc[...] = alpha * acc_sc[...] + pv
        m_sc[...] = m_new

    @pl.when(p == 0)
    def _():
        q = q_ref[0].astype(F32)
        qbd = jnp.concatenate([jnp.where(lane_head == h, q, 0.0) for h in range(H)], axis=0)
        qbd_sc[...] = qbd.astype(BF16)
        ln = _dot3_lhs(lfn_ref[0], pre_ref[...]) * LOG2E
        j3 = lax.broadcasted_iota(jnp.int32, (H, T, page), 2)
        t3i = lax.broadcasted_iota(jnp.int32, (H, T, page), 1)
        bias = jnp.where(j3 <= t3i, -ln[:, None, :], -jnp.inf)
        update(knT_ref[0], vnT_ref[0], bias, True)
        tail_sc[...] = jnp.zeros(tail_sc.shape, F32)

    for g in range(group):
        lft = lf_refs[g][0]
        tail = tail_sc[...]
        r = (_dot3_lhs(lft, suf_ref[...]) + tail) * LOG2E
        update(k_refs[g][0], v_refs[g][0], r[:, None, :], False)
        tail_sc[...] = tail + jnp.sum(lft, axis=1, keepdims=True)

    @pl.when(p == pl.num_programs(1) - 1)
    def _():
        o3 = acc_sc[...] / l_sc[...]
        o = jnp.zeros((T, fw), F32)
        for h in range(H):
            o = o + jnp.where(lane_head == h, o3[h], 0.0)
        o_ref[0] = o


def _fox_sample(page_table, q, knT, vnT, lfn, cache_kT, cache_vT, lft_pool, *, n_heads, n_new):
    DB, n_pages = page_table.shape
    _, fw, page = cache_kT.shape
    H, T = n_heads, n_new
    dh = fw // H
    assert page == V7X_LANES
    group = next(g for g in (8, 4, 2, 1) if n_pages % g == 0)
    suffix = _tri(page, lambda i, j: i > j)
    prefix = _tri(page, lambda i, j: i <= j)
    pt_flat = page_table.reshape(-1).astype(jnp.int32)

    def page_idx(g):
        def index_map(b, p, pt):
            return (pt[b * n_pages + (n_pages - 1 - (p * group + g))], 0, 0)
        return index_map

    per_b = lambda shape: pl.BlockSpec((1,) + shape, lambda b, p, pt: (b, 0, 0))
    const2 = pl.BlockSpec((page, page), lambda b, p, pt: (0, 0))
    in_specs = ([per_b((T, fw)), per_b((fw, page)), per_b((fw, page)), per_b((H, page)), const2, const2]
                + [pl.BlockSpec((1, fw, page), page_idx(g)) for g in range(group)]
                + [pl.BlockSpec((1, fw, page), page_idx(g)) for g in range(group)]
                + [pl.BlockSpec((1, H, page), page_idx(g)) for g in range(group)])
    grid_spec = pltpu.PrefetchScalarGridSpec(
        num_scalar_prefetch=1, grid=(DB, n_pages // group), in_specs=in_specs,
        out_specs=per_b((T, fw)),
        scratch_shapes=[pltpu.VMEM((H * T, fw), BF16), pltpu.VMEM((H, T, 1), F32), pltpu.VMEM((H, T, 1), F32),
                        pltpu.VMEM((H, T, fw), F32), pltpu.VMEM((H, 1), F32)])
    vmem = 2 * 2 * group * fw * page * 4 + 2 * group * V7X_SUBLANES * page * 4 + 32 * H * T * fw * 4 + (8 << 20)
    return pl.pallas_call(
        functools.partial(_fox_sample_body, n_heads=H, n_new=T, dh=dh, group=group),
        grid_spec=grid_spec, out_shape=jax.ShapeDtypeStruct((DB, T, fw), F32),
        compiler_params=_params(("arbitrary", "arbitrary"), vmem),
        name="fox_sample",
    )(pt_flat, q, knT, vnT, lfn, suffix, prefix, *([cache_kT] * group), *([cache_vT] * group), *([lft_pool] * group))


def _mlstm_body(q_ref, k_ref, v_ref, og_ref, g_ref, nw_ref, c0_ref, n0_ref, m0_ref,
                mn_ref, c_ref, n_ref, m_ref, *, n_heads, mdh, ig_col, lf_col):
    c = pl.program_id(1)
    L = q_ref.shape[1]

    @pl.when(c == 0)
    def _():
        c_ref[...] = c0_ref[...]
        n_ref[...] = n0_ref[...]
        m_ref[...] = m0_ref[...]

    G = g_ref[0]
    ri = lax.broadcasted_iota(jnp.int32, (L, L), 0)
    ci = lax.broadcasted_iota(jnp.int32, (L, L), 1)
    causal = ci <= ri
    Bc = _dot3_rhs(causal.astype(BF16), G)
    GT = G.T
    BcT = Bc.T
    for h in range(n_heads):
        sl = slice(h * mdh, (h + 1) * mdh)
        q = q_ref[0, :, sl]
        k = k_ref[0, :, sl]
        v = v_ref[0, :, sl]
        b_col = Bc[:, lf_col + h:lf_col + h + 1]
        b_row = BcT[lf_col + h:lf_col + h + 1, :]
        ig_colv = G[:, ig_col + h:ig_col + h + 1]
        ig_row = GT[ig_col + h:ig_col + h + 1, :]
        m_prev = m_ref[0, h][:, 0:1]
        C = c_ref[0, h]
        n = n_ref[0, h]
        Dm = jnp.where(causal, (b_col - b_row) + ig_row, -jnp.inf)
        inter = m_prev + b_col
        m_t = jnp.maximum(inter, jnp.max(Dm, axis=1, keepdims=True))
        W = jnp.exp(Dm - m_t)
        a = jnp.exp(inter - m_t)
        s = lax.dot_general(q, k, (((1,), (1,)), ((), ())), preferred_element_type=F32)
        Wqk = W * s
        num = (jnp.dot(Wqk.astype(BF16), v, preferred_element_type=F32)
               + jnp.dot(q, C.astype(BF16), preferred_element_type=F32) * a)
        nq = jnp.sum(Wqk, axis=1, keepdims=True) + jnp.sum(q.astype(F32) * n, axis=1, keepdims=True) * a
        den = jnp.maximum(jnp.abs(nq), jnp.exp(-m_t))
        hh = num / den
        mu = jnp.mean(hh, axis=1, keepdims=True)
        d = hh - mu
        var = jnp.mean(d * d, axis=1, keepdims=True)
        mn = d * lax.rsqrt(var + LN_EPS) * nw_ref[:, sl] * og_ref[0, :, sl].astype(F32)
        mn_ref[0, :, sl] = mn.astype(mn_ref.dtype)
        m_end = m_t[L - 1:L, :]
        g_state = jnp.exp(inter[L - 1:L, :] - m_end)
        wk = jnp.exp((b_col[L - 1:L, :] - b_col) + ig_colv - m_end)
        kw = k.astype(F32) * wk
        c_ref[0, h] = g_state * C + jnp.dot(kw.T.astype(BF16), v, preferred_element_type=F32)
        n_ref[0, h] = g_state * n + jnp.sum(kw, axis=0, keepdims=True)
        m_ref[0, h] = jnp.broadcast_to(m_end, (1, V7X_LANES))


def _mlstm(q, k, v, og, gates, norm_w, c0, n0, m0, *, n_heads, ig_col, lf_col, chunk):
    B, S, mw = q.shape
    mdh = mw // n_heads
    assert S % chunk == 0 and mdh == V7X_LANES
    tile = lambda width: pl.BlockSpec((1, chunk, width), lambda b, c: (b, c, 0))
    st = lambda shape: pl.BlockSpec((1,) + shape, lambda b, c: (b, 0, 0, 0))
    c_s, n_s, m_s = st((n_heads, mdh, mdh)), st((n_heads, 1, mdh)), st((n_heads, 1, V7X_LANES))
    vmem = 2 * 5 * chunk * mw * 2 + 2 * chunk * GATE_COLS * 4 + 6 * n_heads * mdh * mdh * 4 + 64 * chunk * chunk * 4 + (8 << 20)
    return pl.pallas_call(
        functools.partial(_mlstm_body, n_heads=n_heads, mdh=mdh, ig_col=ig_col, lf_col=lf_col),
        grid=(B, S // chunk),
        out_shape=[jax.ShapeDtypeStruct((B, S, mw), BF16),
                   jax.ShapeDtypeStruct((B, n_heads, mdh, mdh), F32),
                   jax.ShapeDtypeStruct((B, n_heads, 1, mdh), F32),
                   jax.ShapeDtypeStruct((B, n_heads, 1, V7X_LANES), F32)],
        in_specs=[tile(mw), tile(mw), tile(mw), tile(mw), tile(GATE_COLS), _const_spec((1, mw)), c_s, n_s, m_s],
        out_specs=[tile(mw), c_s, n_s, m_s],
        compiler_params=_params(("arbitrary", "arbitrary"), vmem),
        name="mlstm",
    )(q, k, v, og, gates, norm_w, c0, n0, m0)


def _layer_norm(u, g, b):
    mu = jnp.mean(u, axis=-1, keepdims=True)
    d = u - mu
    var = jnp.mean(d * d, axis=-1, keepdims=True)
    return d * lax.rsqrt(var + LN_EPS) * g + b


def _block_out_body(x_ref, fox_ref, mn_ref, wo_ref, g1_ref, b1_ref, w1_ref, w2_ref, g2_ref, b2_ref, y_ref,
                    *, alpha, ff_chunk):
    fw = fox_ref.shape[1]
    x = x_ref[...]
    mix = (jnp.dot(fox_ref[...], wo_ref[0:fw, :], preferred_element_type=F32)
           + jnp.dot(mn_ref[...], wo_ref[fw:, :], preferred_element_type=F32))
    x1 = _layer_norm(alpha * x + mix, g1_ref[...], b1_ref[...])
    x1b = x1.astype(BF16)
    acc = jnp.zeros(x.shape, F32)
    for c in range(0, w1_ref.shape[1], ff_chunk):
        hid = jnp.dot(x1b, w1_ref[:, c:c + ff_chunk], preferred_element_type=F32)
        hid = jnp.square(jnp.maximum(hid, 0.0)).astype(BF16)
        acc = acc + jnp.dot(hid, w2_ref[c:c + ff_chunk, :], preferred_element_type=F32)
    y_ref[...] = _layer_norm(alpha * x1 + acc, g2_ref[...], b2_ref[...])


def _block_out(x, fox_h, mn, wo, g1, b1, w1, w2, g2, b2, *, alpha, tm):
    N, D = x.shape
    fw, mw, dff = fox_h.shape[1], mn.shape[1], w1.shape[1]
    assert N % tm == 0
    ff_chunk = min(dff, 1024)
    tile = lambda width: pl.BlockSpec((tm, width), lambda i: (i, 0))
    vec = _const_spec((1, D))
    vmem = (4 * tm * D * 4 + 4 * tm * (fw + mw) * 2 + (wo.size + w1.size + w2.size) * 2
            + 6 * tm * D * 4 + 3 * tm * ff_chunk * 4 + (8 << 20))
    return pl.pallas_call(
        functools.partial(_block_out_body, alpha=alpha, ff_chunk=ff_chunk), grid=(N // tm,),
        out_shape=jax.ShapeDtypeStruct((N, D), F32),
        in_specs=[tile(D), tile(fw), tile(mw), _const_spec(wo.shape), vec, vec,
                  _const_spec(w1.shape), _const_spec(w2.shape), vec, vec],
        out_specs=tile(D),
        compiler_params=_params(("arbitrary",), vmem),
        name="block_out",
    )(x, fox_h, mn, wo, g1, b1, w1, w2, g2, b2)


def _pad_axis(a, axis, total, value=0.0):
    pads = [(0, 0)] * a.ndim
    pads[axis] = (0, total - a.shape[axis])
    return jnp.pad(a, pads, constant_values=value)


def kernel(x_prompt, x_sample, cache_k, cache_v, cache_logf, state_C, state_n, state_m, state_conv, page_table,
           w_in, b_fox_f, b_ig, b_fg, b_og, conv_w, conv_b, mlstm_norm_w, w_o, ln1_g, ln1_b, w1, w2, ln2_g, ln2_b):
    depth = w_in.shape[0]
    B, S, D = x_prompt.shape
    DB, T, _ = x_sample.shape
    n_pool, page, H, dh = cache_k.shape[1:]
    MH = b_ig.shape[1]
    fw, mw = H * dh, b_og.shape[1]
    mdh = mw // MH
    cwid = conv_w.shape[1]
    alpha = (2 * depth) ** 0.25
    assert H + 2 * MH <= GATE_COLS and T >= cwid - 1 and S >= cwid - 1 and T <= page
    ig_col, lf_col = H, H + MH

    o_q, o_k, o_v, o_f = 0, fw, 2 * fw, 3 * fw
    o_mqk = o_f + H
    o_mv = o_mqk + 2 * mw
    o_mi = o_mv + mw
    o_mf = o_mi + MH
    o_mo = o_mf + MH
    assert w_in.shape[2] == o_mo + mw

    tile_s = min(S, 512)
    xp, xs = x_prompt, x_sample
    outs = [[] for _ in range(14)]
    for l in range(depth):
        wl = w_in[l]
        gate_w = jnp.concatenate([wl[:, o_f:o_mqk], wl[:, o_mi:o_mo],
                                  jnp.zeros((D, GATE_COLS - H - 2 * MH), wl.dtype)], axis=1)
        w_all = jnp.concatenate([wl[:, o_q:o_f], wl[:, o_mqk:o_mi], wl[:, o_mo:], gate_w], axis=1).astype(BF16)
        b_gate = jnp.concatenate([b_fox_f[l], b_ig[l], b_fg[l], jnp.zeros((GATE_COLS - H - 2 * MH,), F32)])[None, :]
        inproj = functools.partial(_inproj, w_all=w_all, b_gate=b_gate, b_og=b_og[l][None, :], conv_w=conv_w[l],
                                   conv_b=conv_b[l][None, :], fw=fw, mw=mw, n_fox=H, n_ml=MH, dh=dh, mdh=mdh)
        wo_b, w1_b, w2_b = w_o[l].astype(BF16), w1[l].astype(BF16), w2[l].astype(BF16)
        block_out = functools.partial(_block_out, wo=wo_b, g1=ln1_g[l][None, :], b1=ln1_b[l][None, :], w1=w1_b, w2=w2_b,
                                      g2=ln2_g[l][None, :], b2=ln2_b[l][None, :], alpha=alpha)
        mlstm = functools.partial(_mlstm, norm_w=mlstm_norm_w[l][None, :], n_heads=MH, ig_col=ig_col, lf_col=lf_col,
                                  chunk=MLSTM_CHUNK)
        hist = V7X_SUBLANES

        (qT, kb, kT, vT, vTb, gates, mq, mk, mv, og, conv_tail) = inproj(
            xp, conv_prev8=jnp.zeros((B, hist, 2 * mw), F32), nb=1, T=tile_s, feature_major=True)
        lcum = _cumsum_rows(jnp.swapaxes(gates[:, :, :H], 1, 2))
        fox_h = _fox_prompt(qT, kb, vTb, lcum.reshape(B, H // 2, 2, S), dh=dh, tq=tile_s)
        mn, C_p, n_p, m_p = mlstm(mq, mk, mv, og, gates,
                                  c0=jnp.zeros((B, MH, mdh, mdh), F32), n0=jnp.zeros((B, MH, 1, mdh), F32),
                                  m0=jnp.zeros((B, MH, 1, V7X_LANES), F32))
        xp = block_out(xp.reshape(B * S, D), fox_h.reshape(B * S, fw), mn.reshape(B * S, mw), tm=tile_s).reshape(B, S, D)
        to_tokens = lambda aT: jnp.transpose(aT.reshape(B, H, dh, S), (0, 3, 1, 2))
        outs[0].append(to_tokens(kT)); outs[1].append(to_tokens(vT))
        outs[2].append(gates[:, :, :H]); outs[3].append(C_p); outs[4].append(n_p[:, :, 0, :])
        outs[5].append(m_p[:, :, 0, 0]); outs[6].append(conv_tail[:, hist - (cwid - 1):, :])

        prev8 = jnp.pad(state_conv[l], ((0, 0), (hist - (cwid - 1), 0), (0, 0)))
        (sq, skf, svf, sgates, smq, smk, smv, sog, sconv_tail) = inproj(
            xs, conv_prev8=prev8, nb=DB, T=T, feature_major=False)
        new_T = lambda a: _pad_axis(jnp.swapaxes(a, 1, 2), 2, page).astype(BF16)
        lfn = _pad_axis(jnp.swapaxes(sgates[:, :, :H], 1, 2), 2, page)
        cache_kT = jnp.transpose(cache_k[l], (0, 2, 3, 1)).reshape(n_pool, fw, page)
        cache_vT = jnp.transpose(cache_v[l], (0, 2, 3, 1)).reshape(n_pool, fw, page)
        lft_pool = jnp.swapaxes(cache_logf[l], 1, 2)
        o_s = _fox_sample(page_table, sq, new_T(skf), new_T(svf), lfn, cache_kT, cache_vT, lft_pool,
                          n_heads=H, n_new=T)
        sfox_h = o_s.reshape(DB * T, fw).astype(BF16)
        pad_gate = jnp.zeros((GATE_COLS,), F32).at[ig_col:ig_col + MH].set(PAD_IGATE)
        sg_pad = jnp.concatenate([sgates, jnp.broadcast_to(pad_gate, (DB, MLSTM_CHUNK - T, GATE_COLS))], axis=1)
        pr = lambda a: _pad_axis(a, 1, MLSTM_CHUNK)
        smn, C_s, n_s, m_s = mlstm(pr(smq), pr(smk), pr(smv), pr(sog), sg_pad,
                                   c0=state_C[l].astype(F32), n0=state_n[l].astype(F32)[:, :, None, :],
                                   m0=jnp.broadcast_to(state_m[l].astype(F32)[:, :, None, None], (DB, MH, 1, V7X_LANES)))
        xs = block_out(xs.reshape(DB * T, D), sfox_h, smn[:, :T, :].reshape(DB * T, mw), tm=DB * T).reshape(DB, T, D)
        outs[7].append(skf.reshape(DB, T, H, dh)); outs[8].append(svf.reshape(DB, T, H, dh))
        outs[9].append(sgates[:, :, :H]); outs[10].append(C_s); outs[11].append(n_s[:, :, 0, :])
        outs[12].append(m_s[:, :, 0, 0]); outs[13].append(sconv_tail[:, hist - (cwid - 1):, :])

    stacked = [jnp.stack(o) for o in outs]
    return (xp, xs, *stacked)
```

```python
import functools
import math

import jax
import jax.numpy as jnp
from jax import lax
from jax.experimental import pallas as pl
from jax.experimental.pallas import tpu as pltpu

F32 = jnp.float32
BF16 = jnp.bfloat16

V7X_VMEM_BYTES = 64 * 1024 * 1024
V7X_LANES = 128
V7X_SUBLANES = 8
LN_EPS = 1e-5
LOG2E = math.log2(math.e)
GATE_COLS = V7X_LANES
MLSTM_CHUNK = 128
PAD_IGATE = -1e30


def _params(semantics, vmem_bytes):
    return pltpu.CompilerParams(dimension_semantics=semantics,
                                vmem_limit_bytes=int(min(vmem_bytes, V7X_VMEM_BYTES - (4 << 20))))


def _const_spec(shape):
    zeros = (0,) * len(shape)
    return pl.BlockSpec(shape, lambda *_: zeros, pipeline_mode=pl.Buffered(1))


def _log_sigmoid(z):
    return jnp.minimum(z, 0.0) - jnp.log1p(jnp.exp(-jnp.abs(z)))


def _split3(a):
    hi = a.astype(BF16)
    r1 = a - hi.astype(F32)
    mid = r1.astype(BF16)
    lo = (r1 - mid.astype(F32)).astype(BF16)
    return hi, mid, lo


def _dot3_lhs(a, b_bf16):
    return sum(jnp.dot(p, b_bf16, preferred_element_type=F32) for p in _split3(a))


def _dot3_rhs(a_bf16, b):
    return sum(jnp.dot(a_bf16, p, preferred_element_type=F32) for p in _split3(b))


def _tri(n, cmp):
    i = lax.broadcasted_iota(jnp.int32, (n, n), 0)
    j = lax.broadcasted_iota(jnp.int32, (n, n), 1)
    return cmp(i, j).astype(BF16)


def _inproj_body(x_ref, w_ref, bg_ref, bog_ref, cw_ref, cb_ref, cprev_ref, *refs,
                 fw, mw, n_fox, n_ml, q_scale, mk_scale, conv_width, feature_major):
    if feature_major:
        (qT_ref, kb_ref, kT_ref, vT_ref, vTb_ref, g_ref, mq_ref, mk_ref, mv_ref, og_ref, cnew_ref, zbuf_ref) = refs
    else:
        (q_ref, kf_ref, vf_ref, g_ref, mq_ref, mk_ref, mv_ref, og_ref, cnew_ref, zbuf_ref) = refs
    nb, T, D = x_ref.shape
    s = pl.program_id(1)
    x = x_ref[...].reshape(nb * T, D).astype(BF16)

    def proj(lo, width):
        return jnp.dot(x, w_ref[:, lo:lo + width], preferred_element_type=F32)

    off_k, off_v, off_qk = fw, 2 * fw, 3 * fw
    off_mv = off_qk + 2 * mw
    off_o = off_mv + mw
    off_g = off_o + mw

    zq = proj(0, fw) * q_scale
    zk = proj(off_k, fw)
    zv = proj(off_v, fw)
    if feature_major:
        qT_ref[0] = zq.T.astype(BF16)
        kb_ref[...] = zk.astype(BF16).reshape(nb, T, fw)
        kT_ref[0] = zk.T
        vT = zv.T
        vT_ref[0] = vT
        vTb_ref[0] = vT.astype(BF16)
    else:
        q_ref[...] = zq.astype(BF16).reshape(nb, T, fw)
        kf_ref[...] = zk.reshape(nb, T, fw)
        vf_ref[...] = zv.reshape(nb, T, fw)
    mv_ref[...] = proj(off_mv, mw).astype(BF16).reshape(nb, T, mw)
    og_ref[...] = jax.nn.sigmoid(proj(off_o, mw) + bog_ref[...]).astype(BF16).reshape(nb, T, mw)

    zg = proj(off_g, GATE_COLS) + bg_ref[...]
    col = lax.broadcasted_iota(jnp.int32, (1, GATE_COLS), 1)
    is_igate = (col >= n_fox) & (col < n_fox + n_ml)
    g_ref[...] = jnp.where(is_igate, zg, _log_sigmoid(zg)).reshape(nb, T, GATE_COLS)

    zqk = proj(off_qk, 2 * mw).reshape(nb, T, 2 * mw)
    hist = V7X_SUBLANES

    @pl.when(s == 0)
    def _():
        zbuf_ref[:, 0:hist, :] = cprev_ref[...]

    zbuf_ref[:, hist:hist + T, :] = zqk
    conv = cb_ref[...] + zqk * cw_ref[conv_width - 1:conv_width, :]
    for j in range(conv_width - 1):
        start = hist - (conv_width - 1) + j
        conv = conv + zbuf_ref[:, start:start + T, :] * cw_ref[j:j + 1, :]
    qk = conv * jax.nn.sigmoid(conv)
    mq_ref[...] = qk[:, :, :mw].astype(BF16)
    mk_ref[...] = (qk[:, :, mw:] * mk_scale).astype(BF16)
    tail = zbuf_ref[:, T:T + hist, :]
    cnew_ref[...] = tail
    zbuf_ref[:, 0:hist, :] = tail


def _inproj(x, w_all, b_gate, b_og, conv_w, conv_b, conv_prev8, *, fw, mw, n_fox, n_ml, dh, mdh, nb, T,
            feature_major):
    B, S, D = x.shape
    ncols = w_all.shape[1]
    conv_width = conv_w.shape[0]
    assert B % nb == 0 and S % T == 0 and T % V7X_SUBLANES == 0 and T >= V7X_SUBLANES
    grid = (B // nb, S // T)
    tile = lambda width: pl.BlockSpec((nb, T, width), lambda b, s: (b, s, 0))
    per_seq = pl.BlockSpec((nb, V7X_SUBLANES, 2 * mw), lambda b, s: (b, 0, 0))
    tok = lambda width, dt: jax.ShapeDtypeStruct((B, S, width), dt)
    if feature_major:
        assert nb == 1 and T % V7X_LANES == 0
        feat = lambda dt: jax.ShapeDtypeStruct((B, fw, S), dt)
        feat_spec = pl.BlockSpec((1, fw, T), lambda b, s: (b, 0, s))
        fox_shapes = [feat(BF16), tok(fw, BF16), feat(F32), feat(F32), feat(BF16)]
        fox_specs = [feat_spec, tile(fw), feat_spec, feat_spec, feat_spec]
    else:
        fox_shapes = [tok(fw, BF16), tok(fw, F32), tok(fw, F32)]
        fox_specs = [tile(fw)] * 3
    out_shapes = fox_shapes + [tok(GATE_COLS, F32), tok(mw, BF16), tok(mw, BF16), tok(mw, BF16), tok(mw, BF16),
                               jax.ShapeDtypeStruct((B, V7X_SUBLANES, 2 * mw), F32)]
    out_specs = fox_specs + [tile(GATE_COLS), tile(mw), tile(mw), tile(mw), tile(mw), per_seq]
    rows = nb * T
    vmem = (2 * rows * D * 4 + D * ncols * 2 + 2 * rows * (16 * fw + GATE_COLS * 4 + 4 * mw * 2)
            + nb * (T + 8) * 2 * mw * 4 + 6 * rows * 2 * mw * 4 + (8 << 20))
    body = functools.partial(_inproj_body, fw=fw, mw=mw, n_fox=n_fox, n_ml=n_ml,
                             q_scale=dh ** -0.5 * LOG2E, mk_scale=mdh ** -0.5, conv_width=conv_width,
                             feature_major=feature_major)
    return pl.pallas_call(
        body, grid=grid, out_shape=out_shapes,
        in_specs=[tile(D), _const_spec((D, ncols)), _const_spec((1, GATE_COLS)), _const_spec((1, mw)),
                  _const_spec(conv_w.shape), _const_spec((1, 2 * mw)), per_seq],
        out_specs=out_specs,
        scratch_shapes=[pltpu.VMEM((nb, T + V7X_SUBLANES, 2 * mw), F32)],
        compiler_params=_params(("arbitrary", "arbitrary"), vmem),
        name="inproj",
    )(x, w_all, b_gate, b_og, conv_w, conv_b, conv_prev8)


def _cumsum_body(x_ref, o_ref):
    R, S = x_ref.shape[1], x_ref.shape[2]
    L = V7X_LANES
    upper = _tri(L, lambda i, j: i <= j)
    carry = jnp.zeros((R, 1), F32)
    for c in range(S // L):
        cs = _dot3_lhs(x_ref[0, :, c * L:(c + 1) * L], upper) + carry
        o_ref[0, :, c * L:(c + 1) * L] = cs
        carry = cs[:, L - 1:L]


def _cumsum_rows(x):
    B, R, S = x.shape
    assert S % V7X_LANES == 0
    spec = pl.BlockSpec((1, R, S), lambda b: (b, 0, 0))
    return pl.pallas_call(
        _cumsum_body, grid=(B,), out_shape=jax.ShapeDtypeStruct((B, R, S), F32),
        in_specs=[spec], out_specs=spec,
        compiler_params=_params(("arbitrary",), 8 * R * S * 4 + (8 << 20)),
        name="cumsum_rows",
    )(x)


def _fox_prompt_body(qT_ref, k_ref, vT_ref, l_ref, o_ref, lcol_sc, m_sc, l_sc, acc_sc, *, tq, dh):
    qi = pl.program_id(2)
    S = k_ref.shape[1]
    L = V7X_LANES

    @pl.when(qi == 0)
    def _():
        for a in range(2):
            for c in range(S // L):
                row = l_ref[0, 0, a:a + 1, c * L:(c + 1) * L] * LOG2E
                lcol_sc[a, c * L:(c + 1) * L, :] = jnp.broadcast_to(row, (L, L)).T

    qT = qT_ref[0]
    frow = lax.broadcasted_iota(jnp.int32, (2 * dh, 1), 0)
    zero = jnp.zeros_like(qT)
    qTs = [jnp.where((frow >= a * dh) & (frow < (a + 1) * dh), qT, zero) for a in range(2)]
    m_sc[...] = jnp.full(m_sc.shape, -jnp.inf, F32)
    l_sc[...] = jnp.zeros(l_sc.shape, F32)
    acc_sc[...] = jnp.zeros(acc_sc.shape, F32)
    key_i = lax.broadcasted_iota(jnp.int32, (tq, tq), 0)
    qry_j = lax.broadcasted_iota(jnp.int32, (tq, tq), 1)

    def block(ki, masked):
        off = pl.multiple_of(ki * tq, tq)
        k = k_ref[0, pl.ds(off, tq), :]
        vT = vT_ref[0, :, pl.ds(off, tq)]
        for a in range(2):
            sT = jnp.dot(k, qTs[a], preferred_element_type=F32)
            lc = lcol_sc[a, pl.ds(off, tq), :]
            t = sT - jnp.concatenate([lc] * (tq // L), axis=1)
            if masked:
                t = jnp.where(key_i <= qry_j, t, -jnp.inf)
            m_prev = m_sc[a]
            m_new = jnp.maximum(m_prev, jnp.max(t, axis=0, keepdims=True))
            alpha = jnp.exp2(m_prev - m_new)
            p = jnp.exp2(t - m_new)
            l_sc[a] = alpha * l_sc[a] + jnp.sum(p, axis=0, keepdims=True)
            pv = jnp.dot(vT[a * dh:(a + 1) * dh, :], p.astype(BF16), preferred_element_type=F32)
            acc_sc[a] = alpha * acc_sc[a] + pv
            m_sc[a] = m_new

    def loop_body(ki, carry):
        block(ki, False)
        return carry

    lax.fori_loop(0, qi, loop_body, 0)
    block(qi, True)
    oT = jnp.concatenate([acc_sc[0] / l_sc[0], acc_sc[1] / l_sc[1]], axis=0)
    o_ref[0] = oT.T.astype(o_ref.dtype)


def _fox_prompt(qT, k, vT, lcum, *, dh, tq):
    B, S, fw = k.shape
    pair = 2 * dh
    assert pair == V7X_LANES and fw % pair == 0 and S % tq == 0 and tq % V7X_LANES == 0
    grid = (B, fw // pair, S // tq)
    vmem = (2 * 2 * S * pair * 2 + 4 * tq * pair * 2 + 2 * 2 * S * 4 * 8 + 2 * S * V7X_LANES * 4
            + 4 * pair * tq * 4 + 10 * tq * tq * 4 + (8 << 20))
    return pl.pallas_call(
        functools.partial(_fox_prompt_body, tq=tq, dh=dh), grid=grid,
        out_shape=jax.ShapeDtypeStruct((B, S, fw), BF16),
        in_specs=[pl.BlockSpec((1, pair, tq), lambda b, hp, qi: (b, hp, qi)),
                  pl.BlockSpec((1, S, pair), lambda b, hp, qi: (b, 0, hp)),
                  pl.BlockSpec((1, pair, S), lambda b, hp, qi: (b, hp, 0)),
                  pl.BlockSpec((1, 1, 2, S), lambda b, hp, qi: (b, hp, 0, 0))],
        out_specs=pl.BlockSpec((1, tq, pair), lambda b, hp, qi: (b, qi, hp)),
        scratch_shapes=[pltpu.VMEM((2, S, V7X_LANES), F32), pltpu.VMEM((2, 1, tq), F32), pltpu.VMEM((2, 1, tq), F32),
                        pltpu.VMEM((2, dh, tq), F32)],
        compiler_params=_params(("arbitrary", "arbitrary", "arbitrary"), vmem),
        name="fox_prompt",
    )(qT, k, vT, lcum)


def _fox_sample_body(pt_ref, q_ref, knT_ref, vnT_ref, lfn_ref, suf_ref, pre_ref, *refs,
                     n_heads, n_new, dh, group):
    del pt_ref
    k_refs, v_refs, lf_refs = refs[0:group], refs[group:2 * group], refs[2 * group:3 * group]
    o_ref, qbd_sc, m_sc, l_sc, acc_sc, tail_sc = refs[3 * group:]
    p = pl.program_id(1)
    H, T = n_heads, n_new
    fw = H * dh
    page = k_refs[0].shape[2]
    lane_head = lax.broadcasted_iota(jnp.int32, (1, fw), 1) // dh

    def update(kT, vT, bias3, first):
        s2 = jnp.dot(qbd_sc[...], kT.astype(BF16), preferred_element_type=F32)
        t3 = s2.reshape(H, T, page) + bias3
        m_blk = jnp.max(t3, axis=2, keepdims=True)
        if first:
            m_new = m_blk
        else:
            m_prev = m_sc[...]
            m_new = jnp.maximum(m_prev, m_blk)
            alpha = jnp.exp2(m_prev - m_new)
        pm = jnp.exp2(t3 - m_new)
        l_blk = jnp.sum(pm, axis=2, keepdims=True)
        pv = lax.dot_general(pm.reshape(H * T, page).astype(BF16), vT.astype(BF16), (((1,), (1,)), ((), ())),
                             preferred_element_type=F32).reshape(H, T, fw)
        if first:
            l_sc[...] = l_blk
            acc_sc[...] = pv
        else:
            l_sc[...] = alpha * l_sc[...] + l_blk
            acc_sc[...] = alpha * acc_sc[...] + pv
        m_sc[...] = m_new

    @pl.when(p == 0)
    def _():
        q = q_ref[0].astype(F32)
        qbd = jnp.concatenate([jnp.where(lane_head == h, q, 0.0) for h in range(H)], axis=0)
        qbd_sc[...] = qbd.astype(BF16)
        ln = _dot3_lhs(lfn_ref[0], pre_ref[...]) * LOG2E
        j3 = lax.broadcasted_iota(jnp.int32, (H, T, page), 2)
        t3i = lax.broadcasted_iota(jnp.int32, (H, T, page), 1)
        bias = jnp.where(j3 <= t3i, -ln[:, None, :], -jnp.inf)
        update(knT_ref[0], vnT_ref[0], bias, True)
        tail_sc[...] = jnp.zeros(tail_sc.shape, F32)

    lfts = [lf_refs[g][0] for g in range(group)]
    within = _dot3_lhs(jnp.concatenate(lfts, axis=0), suf_ref[...])
    tail = tail_sc[...]
    rs = []
    for g in range(group):
        rs.append((within[g * H:(g + 1) * H, :] + tail) * LOG2E)
        tail = tail + jnp.sum(lfts[g], axis=1, keepdims=True)
    tail_sc[...] = tail
    qbd = qbd_sc[...]
    width = group * page
    s2 = jnp.concatenate([jnp.dot(qbd, k_refs[g][0].astype(BF16), preferred_element_type=F32)
                          for g in range(group)], axis=1)
    t3 = s2.reshape(H, T, width) + jnp.concatenate(rs, axis=1)[:, None, :]
    m_prev = m_sc[...]
    m_new = jnp.maximum(m_prev, jnp.max(t3, axis=2, keepdims=True))
    alpha = jnp.exp2(m_prev - m_new)
    pm = jnp.exp2(t3 - m_new)
    l_sc[...] = alpha * l_sc[...] + jnp.sum(pm, axis=2, keepdims=True)
    pmb = pm.reshape(H * T, width).astype(BF16)
    pv = sum(lax.dot_general(pmb[:, g * page:(g + 1) * page], v_refs[g][0].astype(BF16), (((1,), (1,)), ((), ())),
                             preferred_element_type=F32) for g in range(group))
    acc_sc[...] = alpha * acc_sc[...] + pv.reshape(H, T, fw)
    m_sc[...] = m_new

    @pl.when(p == pl.num_programs(1) - 1)
    def _():
        o3 = acc_sc[...] / l_sc[...]
        o = jnp.zeros((T, fw), F32)
        for h in range(H):
            o = o + jnp.where(lane_head == h, o3[h], 0.0)
        o_ref[0] = o


def _fox_sample(page_table, q, knT, vnT, lfn, cache_kT, cache_vT, lft_pool, *, n_heads, n_new):
    DB, n_pages = page_table.shape
    _, fw, page = cache_kT.shape
    H, T = n_heads, n_new
    dh = fw // H
    assert page == V7X_LANES
    group = next(g for g in (8, 4, 2, 1) if n_pages % g == 0)
    suffix = _tri(page, lambda i, j: i > j)
    prefix = _tri(page, lambda i, j: i <= j)
    pt_flat = page_table.reshape(-1).astype(jnp.int32)

    def page_idx(g):
        def index_map(b, p, pt):
            return (pt[b * n_pages + (n_pages - 1 - (p * group + g))], 0, 0)
        return index_map

    per_b = lambda shape: pl.BlockSpec((1,) + shape, lambda b, p, pt: (b, 0, 0))
    const2 = pl.BlockSpec((page, page), lambda b, p, pt: (0, 0))
    in_specs = ([per_b((T, fw)), per_b((fw, page)), per_b((fw, page)), per_b((H, page)), const2, const2]
                + [pl.BlockSpec((1, fw, page), page_idx(g)) for g in range(group)]
                + [pl.BlockSpec((1, fw, page), page_idx(g)) for g in range(group)]
                + [pl.BlockSpec((1, H, page), page_idx(g)) for g in range(group)])
    grid_spec = pltpu.PrefetchScalarGridSpec(
        num_scalar_prefetch=1, grid=(DB, n_pages // group), in_specs=in_specs,
        out_specs=per_b((T, fw)),
        scratch_shapes=[pltpu.VMEM((H * T, fw), BF16), pltpu.VMEM((H, T, 1), F32), pltpu.VMEM((H, T, 1), F32),
                        pltpu.VMEM((H, T, fw), F32), pltpu.VMEM((H, 1), F32)])
    vmem = 2 * 2 * group * fw * page * 4 + 2 * group * V7X_SUBLANES * page * 4 + 32 * H * T * fw * 4 + (8 << 20)
    return pl.pallas_call(
        functools.partial(_fox_sample_body, n_heads=H, n_new=T, dh=dh, group=group),
        grid_spec=grid_spec, out_shape=jax.ShapeDtypeStruct((DB, T, fw), F32),
        compiler_params=_params(("arbitrary", "arbitrary"), vmem),
        name="fox_sample",
    )(pt_flat, q, knT, vnT, lfn, suffix, prefix, *([cache_kT] * group), *([cache_vT] * group), *([lft_pool] * group))


def _mlstm_body(q_ref, k_ref, v_ref, og_ref, g_ref, nw_ref, c0_ref, n0_ref, m0_ref,
                mn_ref, c_ref, n_ref, m_ref, *, n_heads, mdh, ig_col, lf_col):
    c = pl.program_id(1)
    L = q_ref.shape[1]

    @pl.when(c == 0)
    def _():
        c_ref[...] = c0_ref[...]
        n_ref[...] = n0_ref[...]
        m_ref[...] = m0_ref[...]

    G = g_ref[0]
    ri = lax.broadcasted_iota(jnp.int32, (L, L), 0)
    ci = lax.broadcasted_iota(jnp.int32, (L, L), 1)
    causal = ci <= ri
    Bc = _dot3_rhs(causal.astype(BF16), G)
    GT = G.T
    BcT = Bc.T
    for h in range(n_heads):
        sl = slice(h * mdh, (h + 1) * mdh)
        q = q_ref[0, :, sl]
        k = k_ref[0, :, sl]
        v = v_ref[0, :, sl]
        b_col = Bc[:, lf_col + h:lf_col + h + 1]
        b_row = BcT[lf_col + h:lf_col + h + 1, :]
        ig_colv = G[:, ig_col + h:ig_col + h + 1]
        ig_row = GT[ig_col + h:ig_col + h + 1, :]
        m_prev = m_ref[0, h][:, 0:1]
        C = c_ref[0, h]
        n = n_ref[0, h]
        Dm = jnp.where(causal, (b_col - b_row) + ig_row, -jnp.inf)
        inter = m_prev + b_col
        m_t = jnp.maximum(inter, jnp.max(Dm, axis=1, keepdims=True))
        W = jnp.exp(Dm - m_t)
        a = jnp.exp(inter - m_t)
        s = lax.dot_general(q, k, (((1,), (1,)), ((), ())), preferred_element_type=F32)
        Wqk = W * s
        num = (jnp.dot(Wqk.astype(BF16), v, preferred_element_type=F32)
               + jnp.dot(q, C.astype(BF16), preferred_element_type=F32) * a)
        nq = jnp.sum(Wqk, axis=1, keepdims=True) + jnp.sum(q.astype(F32) * n, axis=1, keepdims=True) * a
        den = jnp.maximum(jnp.abs(nq), jnp.exp(-m_t))
        hh = num / den
        mu = jnp.mean(hh, axis=1, keepdims=True)
        d = hh - mu
        var = jnp.mean(d * d, axis=1, keepdims=True)
        mn = d * lax.rsqrt(var + LN_EPS) * nw_ref[:, sl] * og_ref[0, :, sl].astype(F32)
        mn_ref[0, :, sl] = mn.astype(mn_ref.dtype)
        m_end = m_t[L - 1:L, :]
        g_state = jnp.exp(inter[L - 1:L, :] - m_end)
        wk = jnp.exp((b_col[L - 1:L, :] - b_col) + ig_colv - m_end)
        kw = k.astype(F32) * wk
        c_ref[0, h] = g_state * C + jnp.dot(kw.T.astype(BF16), v, preferred_element_type=F32)
        n_ref[0, h] = g_state * n + jnp.sum(kw, axis=0, keepdims=True)
        m_ref[0, h] = jnp.broadcast_to(m_end, (1, V7X_LANES))


def _mlstm(q, k, v, og, gates, norm_w, c0, n0, m0, *, n_heads, ig_col, lf_col, chunk):
    B, S, mw = q.shape
    mdh = mw // n_heads
    assert S % chunk == 0 and mdh == V7X_LANES
    tile = lambda width: pl.BlockSpec((1, chunk, width), lambda b, c: (b, c, 0))
    st = lambda shape: pl.BlockSpec((1,) + shape, lambda b, c: (b, 0, 0, 0))
    c_s, n_s, m_s = st((n_heads, mdh, mdh)), st((n_heads, 1, mdh)), st((n_heads, 1, V7X_LANES))
    vmem = 2 * 5 * chunk * mw * 2 + 2 * chunk * GATE_COLS * 4 + 6 * n_heads * mdh * mdh * 4 + 64 * chunk * chunk * 4 + (8 << 20)
    return pl.pallas_call(
        functools.partial(_mlstm_body, n_heads=n_heads, mdh=mdh, ig_col=ig_col, lf_col=lf_col),
        grid=(B, S // chunk),
        out_shape=[jax.ShapeDtypeStruct((B, S, mw), BF16),
                   jax.ShapeDtypeStruct((B, n_heads, mdh, mdh), F32),
                   jax.ShapeDtypeStruct((B, n_heads, 1, mdh), F32),
                   jax.ShapeDtypeStruct((B, n_heads, 1, V7X_LANES), F32)],
        in_specs=[tile(mw), tile(mw), tile(mw), tile(mw), tile(GATE_COLS), _const_spec((1, mw)), c_s, n_s, m_s],
        out_specs=[tile(mw), c_s, n_s, m_s],
        compiler_params=_params(("arbitrary", "arbitrary"), vmem),
        name="mlstm",
    )(q, k, v, og, gates, norm_w, c0, n0, m0)


def _layer_norm(u, g, b):
    mu = jnp.mean(u, axis=-1, keepdims=True)
    d = u - mu
    var = jnp.mean(d * d, axis=-1, keepdims=True)
    return d * lax.rsqrt(var + LN_EPS) * g + b


def _block_out_body(x_ref, fox_ref, mn_ref, wo_ref, g1_ref, b1_ref, w1_ref, w2_ref, g2_ref, b2_ref, y_ref,
                    *, alpha, ff_chunk):
    fw = fox_ref.shape[1]
    x = x_ref[...]
    mix = (jnp.dot(fox_ref[...], wo_ref[0:fw, :], preferred_element_type=F32)
           + jnp.dot(mn_ref[...], wo_ref[fw:, :], preferred_element_type=F32))
    x1 = _layer_norm(alpha * x + mix, g1_ref[...], b1_ref[...])
    x1b = x1.astype(BF16)
    acc = jnp.zeros(x.shape, F32)
    for c in range(0, w1_ref.shape[1], ff_chunk):
        hid = jnp.dot(x1b, w1_ref[:, c:c + ff_chunk], preferred_element_type=F32)
        hid = jnp.square(jnp.maximum(hid, 0.0)).astype(BF16)
        acc = acc + jnp.dot(hid, w2_ref[c:c + ff_chunk, :], preferred_element_type=F32)
    y_ref[...] = _layer_norm(alpha * x1 + acc, g2_ref[...], b2_ref[...])


def _block_out(x, fox_h, mn, wo, g1, b1, w1, w2, g2, b2, *, alpha, tm):
    N, D = x.shape
    fw, mw, dff = fox_h.shape[1], mn.shape[1], w1.shape[1]
    assert N % tm == 0
    ff_chunk = min(dff, 1024)
    tile = lambda width: pl.BlockSpec((tm, width), lambda i: (i, 0))
    vec = _const_spec((1, D))
    vmem = (4 * tm * D * 4 + 4 * tm * (fw + mw) * 2 + (wo.size + w1.size + w2.size) * 2
            + 6 * tm * D * 4 + 3 * tm * ff_chunk * 4 + (8 << 20))
    return pl.pallas_call(
        functools.partial(_block_out_body, alpha=alpha, ff_chunk=ff_chunk), grid=(N // tm,),
        out_shape=jax.ShapeDtypeStruct((N, D), F32),
        in_specs=[tile(D), tile(fw), tile(mw), _const_spec(wo.shape), vec, vec,
                  _const_spec(w1.shape), _const_spec(w2.shape), vec, vec],
        out_specs=tile(D),
        compiler_params=_params(("arbitrary",), vmem),
        name="block_out",
    )(x, fox_h, mn, wo, g1, b1, w1, w2, g2, b2)


def _pad_axis(a, axis, total, value=0.0):
    pads = [(0, 0)] * a.ndim
    pads[axis] = (0, total - a.shape[axis])
    return jnp.pad(a, pads, constant_values=value)


def kernel(x_prompt, x_sample, cache_k, cache_v, cache_logf, state_C, state_n, state_m, state_conv, page_table,
           w_in, b_fox_f, b_ig, b_fg, b_og, conv_w, conv_b, mlstm_norm_w, w_o, ln1_g, ln1_b, w1, w2, ln2_g, ln2_b):
    depth = w_in.shape[0]
    B, S, D = x_prompt.shape
    DB, T, _ = x_sample.shape
    n_pool, page, H, dh = cache_k.shape[1:]
    MH = b_ig.shape[1]
    fw, mw = H * dh, b_og.shape[1]
    mdh = mw // MH
    cwid = conv_w.shape[1]
    alpha = (2 * depth) ** 0.25
    assert H + 2 * MH <= GATE_COLS and T >= cwid - 1 and S >= cwid - 1 and T <= page
    ig_col, lf_col = H, H + MH

    o_q, o_k, o_v, o_f = 0, fw, 2 * fw, 3 * fw
    o_mqk = o_f + H
    o_mv = o_mqk + 2 * mw
    o_mi = o_mv + mw
    o_mf = o_mi + MH
    o_mo = o_mf + MH
    assert w_in.shape[2] == o_mo + mw

    tile_s = min(S, 512)
    xp, xs = x_prompt, x_sample
    outs = [[] for _ in range(14)]
    for l in range(depth):
        wl = w_in[l]
        gate_w = jnp.concatenate([wl[:, o_f:o_mqk], wl[:, o_mi:o_mo],
                                  jnp.zeros((D, GATE_COLS - H - 2 * MH), wl.dtype)], axis=1)
        w_all = jnp.concatenate([wl[:, o_q:o_f], wl[:, o_mqk:o_mi], wl[:, o_mo:], gate_w], axis=1).astype(BF16)
        b_gate = jnp.concatenate([b_fox_f[l], b_ig[l], b_fg[l], jnp.zeros((GATE_COLS - H - 2 * MH,), F32)])[None, :]
        inproj = functools.partial(_inproj, w_all=w_all, b_gate=b_gate, b_og=b_og[l][None, :], conv_w=conv_w[l],
                                   conv_b=conv_b[l][None, :], fw=fw, mw=mw, n_fox=H, n_ml=MH, dh=dh, mdh=mdh)
        wo_b, w1_b, w2_b = w_o[l].astype(BF16), w1[l].astype(BF16), w2[l].astype(BF16)
        block_out = functools.partial(_block_out, wo=wo_b, g1=ln1_g[l][None, :], b1=ln1_b[l][None, :], w1=w1_b, w2=w2_b,
                                      g2=ln2_g[l][None, :], b2=ln2_b[l][None, :], alpha=alpha)
        mlstm = functools.partial(_mlstm, norm_w=mlstm_norm_w[l][None, :], n_heads=MH, ig_col=ig_col, lf_col=lf_col,
                                  chunk=MLSTM_CHUNK)
        hist = V7X_SUBLANES

        (qT, kb, kT, vT, vTb, gates, mq, mk, mv, og, conv_tail) = inproj(
            xp, conv_prev8=jnp.zeros((B, hist, 2 * mw), F32), nb=1, T=tile_s, feature_major=True)
        lcum = _cumsum_rows(jnp.swapaxes(gates[:, :, :H], 1, 2))
        fox_h = _fox_prompt(qT, kb, vTb, lcum.reshape(B, H // 2, 2, S), dh=dh, tq=tile_s)
        mn, C_p, n_p, m_p = mlstm(mq, mk, mv, og, gates,
                                  c0=jnp.zeros((B, MH, mdh, mdh), F32), n0=jnp.zeros((B, MH, 1, mdh), F32),
                                  m0=jnp.zeros((B, MH, 1, V7X_LANES), F32))
        xp = block_out(xp.reshape(B * S, D), fox_h.reshape(B * S, fw), mn.reshape(B * S, mw), tm=tile_s).reshape(B, S, D)
        to_tokens = lambda aT: jnp.transpose(aT.reshape(B, H, dh, S), (0, 3, 1, 2))
        outs[0].append(to_tokens(kT)); outs[1].append(to_tokens(vT))
        outs[2].append(gates[:, :, :H]); outs[3].append(C_p); outs[4].append(n_p[:, :, 0, :])
        outs[5].append(m_p[:, :, 0, 0]); outs[6].append(conv_tail[:, hist - (cwid - 1):, :])

        prev8 = jnp.pad(state_conv[l], ((0, 0), (hist - (cwid - 1), 0), (0, 0)))
        (sq, skf, svf, sgates, smq, smk, smv, sog, sconv_tail) = inproj(
            xs, conv_prev8=prev8, nb=DB, T=T, feature_major=False)
        new_T = lambda a: _pad_axis(jnp.swapaxes(a, 1, 2), 2, page).astype(BF16)
        lfn = _pad_axis(jnp.swapaxes(sgates[:, :, :H], 1, 2), 2, page)
        cache_kT = jnp.transpose(cache_k[l], (0, 2, 3, 1)).reshape(n_pool, fw, page)
        cache_vT = jnp.transpose(cache_v[l], (0, 2, 3, 1)).reshape(n_pool, fw, page)
        lft_pool = jnp.swapaxes(cache_logf[l], 1, 2)
        o_s = _fox_sample(page_table, sq, new_T(skf), new_T(svf), lfn, cache_kT, cache_vT, lft_pool,
                          n_heads=H, n_new=T)
        sfox_h = o_s.reshape(DB * T, fw).astype(BF16)
        pad_gate = jnp.zeros((GATE_COLS,), F32).at[ig_col:ig_col + MH].set(PAD_IGATE)
        sg_pad = jnp.concatenate([sgates, jnp.broadcast_to(pad_gate, (DB, MLSTM_CHUNK - T, GATE_COLS))], axis=1)
        pr = lambda a: _pad_axis(a, 1, MLSTM_CHUNK)
        smn, C_s, n_s, m_s = mlstm(pr(smq), pr(smk), pr(smv), pr(sog), sg_pad,
                                   c0=state_C[l].astype(F32), n0=state_n[l].astype(F32)[:, :, None, :],
                                   m0=jnp.broadcast_to(state_m[l].astype(F32)[:, :, None, None], (DB, MH, 1, V7X_LANES)))
        xs = block_out(xs.reshape(DB * T, D), sfox_h, smn[:, :T, :].reshape(DB * T, mw), tm=DB * T).reshape(DB, T, D)
        outs[7].append(skf.reshape(DB, T, H, dh)); outs[8].append(svf.reshape(DB, T, H, dh))
        outs[9].append(sgates[:, :, :H]); outs[10].append(C_s); outs[11].append(n_s[:, :, 0, :])
        outs[12].append(m_s[:, :, 0, 0]); outs[13].append(sconv_tail[:, hist - (cwid - 1):, :])

    stacked = [jnp.stack(o) for o in outs]
    return (xp, xs, *stacked)
```
